```python
import jax, jax.numpy as jnp
from jax import lax
import numpy as np

D_MODEL = 1024
BATCH = 8
SEQ = 2048
DEPTH = 4

GRID_W = 64
CTX_LEN = 256
N_MIXERS = 2
N_SSD_LAYERS = (DEPTH + 1) // 2
N_ATTN_LAYERS = DEPTH // 2
EPS = 1e-6

SSM_EXPAND = 2
D_INNER = SSM_EXPAND * D_MODEL
SSM_HEADDIM = 64
SSM_HEADS = D_INNER // SSM_HEADDIM
SSM_STATE = 128
SSM_GROUPS = 4
SSM_HPG = SSM_HEADS // SSM_GROUPS
SSM_CONV = 5
SSM_CHUNK = 128
GN = SSM_GROUPS * SSM_STATE
CONV_DIM = D_INNER + 2 * GN
D_IN_PROJ = 2 * D_INNER + 2 * GN + 2 * SSM_HEADS

HEAD_DIM = 64
N_Q_HEADS = D_MODEL // HEAD_DIM
N_KV_HEADS = 4
GQA_GROUP = N_Q_HEADS // N_KV_HEADS
WINDOW = 128
ATTN_BLOCK = 128
ROPE_BASE = 10000.0
ROPE_PAIRS = HEAD_DIM // 4
QKV_DIM = (N_Q_HEADS + 2 * N_KV_HEADS) * HEAD_DIM

N_EXPERTS = 16
N_EXPERT_GROUPS = 4
EXPERTS_PER_GROUP = N_EXPERTS // N_EXPERT_GROUPS
TOPK_GROUPS = 1
TOP_K = 2
D_EXPERT = 512
D_SHARED = 512

kernel_name = 'hybrid_ssd_swa_moe_dit_prefix'


def rmsnorm(x, w):
    xf = x.astype(jnp.float32)
    y = xf * lax.rsqrt(jnp.mean(xf * xf, axis=-1, keepdims=True) + EPS)
    return y.astype(x.dtype) * w


def modulate(x, shift, scale):
    return x * (1 + scale) + shift


def axial_rope_tables(rows):
    row = jnp.repeat(jnp.arange(rows), GRID_W)
    col = jnp.tile(jnp.arange(GRID_W), rows)
    inv_freq = ROPE_BASE ** (-jnp.arange(ROPE_PAIRS, dtype=jnp.float32) / ROPE_PAIRS)
    ang = jnp.stack([row, col], axis=-1).astype(jnp.float32)[:, :, None] * inv_freq
    return jnp.cos(ang), jnp.sin(ang)


def axial_rope(x, cos, sin):
    lead = x.shape[:-1]
    xr = x.reshape(lead + (2, 2, ROPE_PAIRS))
    bshape = (1, x.shape[1]) + (1,) * (x.ndim - 3) + (2, ROPE_PAIRS)
    c = cos.reshape(bshape).astype(x.dtype)
    s = sin.reshape(bshape).astype(x.dtype)
    x1, x2 = xr[..., 0, :], xr[..., 1, :]
    out = jnp.stack([x1 * c - x2 * s, x2 * c + x1 * s], axis=-2)
    return out.reshape(x.shape)


def centred_dwconv(u, w, b):
    pad = (SSM_CONV - 1) // 2
    y = lax.conv_general_dilated(u, w[:, None, :].astype(u.dtype), window_strides=(1,),
                                 padding=[(pad, pad)], dimension_numbers=('NWC', 'WIO', 'NWC'),
                                 feature_group_count=u.shape[-1])
    return y + b


def ssd_chunked_scan(xs, dt, a_neg, bm, cm, h0, need_y):
    f32 = jnp.float32
    bsz, L = xs.shape[0], xs.shape[1]
    nc = L // SSM_CHUNK
    x_c = xs.astype(f32).reshape(bsz, nc, SSM_CHUNK, SSM_GROUPS, SSM_HPG, SSM_HEADDIM)
    dt_c = dt.astype(f32).reshape(bsz, nc, SSM_CHUNK, SSM_GROUPS, SSM_HPG)
    b_c = bm.astype(f32).reshape(bsz, nc, SSM_CHUNK, SSM_GROUPS, SSM_STATE)
    c_c = cm.astype(f32).reshape(bsz, nc, SSM_CHUNK, SSM_GROUPS, SSM_STATE)
    cum = jnp.cumsum(dt_c * a_neg.astype(f32).reshape(SSM_GROUPS, SSM_HPG), axis=2)
    dtx = dt_c[..., None] * x_c
    decay_end = jnp.exp(cum[:, :, -1:] - cum)
    states = jnp.einsum('bcjgn,bcjgh,bcjghp->bcghpn', b_c, decay_end, dtx)
    chunk_decay = jnp.exp(cum[:, :, -1])

    def step(h, inp):
        st, dec = inp
        return h * dec[..., None, None] + st, h

    final, h_prev = lax.scan(step, h0.astype(f32),
                             (jnp.moveaxis(states, 1, 0), jnp.moveaxis(chunk_decay, 1, 0)))
    if not need_y:
        return None, final
    h_prev = jnp.moveaxis(h_prev, 0, 1)
    idx = jnp.arange(SSM_CHUNK)
    causal = (idx[:, None] >= idx[None, :])[None, None, :, :, None, None]
    seg = cum[:, :, :, None] - cum[:, :, None, :]
    decay_in = jnp.exp(jnp.where(causal, seg, -jnp.inf))
    cb = jnp.einsum('bcign,bcjgn->bcijg', c_c, b_c)
    y_diag = jnp.einsum('bcijg,bcijgh,bcjghp->bcighp', cb, decay_in, dtx)
    y_off = jnp.einsum('bcign,bcghpn,bcigh->bcighp', c_c, h_prev, jnp.exp(cum))
    y = (y_diag + y_off).reshape(bsz, L, SSM_HEADS, SSM_HEADDIM)
    return y.astype(xs.dtype), final


def ssd_mixer(h_lat, h_ctx, in_w, conv_w, conv_b, dt_bias, a_log, d_skip, norm_w, out_w, need_ctx_out):
    bsz = h_lat.shape[0]
    a_f = -jnp.exp(a_log[0])
    a_b = -jnp.exp(a_log[1])

    def project(h):
        L = h.shape[1]
        zxbcdt = h @ in_w
        z = zxbcdt[..., :D_INNER]
        xbc = jax.nn.silu(centred_dwconv(zxbcdt[..., D_INNER:D_INNER + CONV_DIM], conv_w, conv_b))
        dt_raw = zxbcdt[..., D_INNER + CONV_DIM:]
        xs = xbc[..., :D_INNER].reshape(bsz, L, SSM_HEADS, SSM_HEADDIM)
        bm = xbc[..., D_INNER:D_INNER + GN].reshape(bsz, L, SSM_GROUPS, SSM_STATE)
        cm = xbc[..., D_INNER + GN:].reshape(bsz, L, SSM_GROUPS, SSM_STATE)
        dt_f = jax.nn.softplus(dt_raw[..., :SSM_HEADS] + dt_bias[0])
        dt_b = jax.nn.softplus(dt_raw[..., SSM_HEADS:] + dt_bias[1])
        return z, xs, bm, cm, dt_f, dt_b

    def bidirectional(z, xs, bm, cm, dt_f, dt_b, h0_f, h0_b, need_y):
        L = xs.shape[1]
        y_f, fin_f = ssd_chunked_scan(xs, dt_f, a_f, bm, cm, h0_f, need_y)
        y_b, fin_b = ssd_chunked_scan(jnp.flip(xs, 1), jnp.flip(dt_b, 1), a_b,
                                      jnp.flip(bm, 1), jnp.flip(cm, 1), h0_b, need_y)
        if not need_y:
            return None, fin_f, fin_b
        y = y_f + jnp.flip(y_b, 1) + xs * d_skip[:, None]
        y = rmsnorm(y.reshape(bsz, L, D_INNER) * jax.nn.silu(z), norm_w)
        return y @ out_w, fin_f, fin_b

    zeros = jnp.zeros((bsz, SSM_GROUPS, SSM_HPG, SSM_HEADDIM, SSM_STATE), jnp.float32)
    out_c, fin_f, fin_b = bidirectional(*project(h_ctx), zeros, zeros, need_ctx_out)
    out_l, _, _ = bidirectional(*project(h_lat), fin_f, fin_b, True)
    return out_l, out_c


def attn_mixer(h_lat, h_ctx, qkv_w, sink, out_w, cos, sin, need_ctx_out):
    f32 = jnp.float32
    bsz, S = h_lat.shape[0], h_lat.shape[1]
    n_ctx = h_ctx.shape[1]
    nb = S // ATTN_BLOCK
    scale = HEAD_DIM ** -0.5
    qd = N_Q_HEADS * HEAD_DIM
    kd = N_KV_HEADS * HEAD_DIM

    def project(h):
        L = h.shape[1]
        qkv = h @ qkv_w
        q = qkv[..., :qd].reshape(bsz, L, N_KV_HEADS, GQA_GROUP, HEAD_DIM)
        k = qkv[..., qd:qd + kd].reshape(bsz, L, N_KV_HEADS, HEAD_DIM)
        v = qkv[..., qd + kd:].reshape(bsz, L, N_KV_HEADS, HEAD_DIM)
        return q, k, v

    q_c, k_c, v_c = project(h_ctx)
    q_l, k_l, v_l = project(h_lat)
    q_l = axial_rope(q_l, cos, sin)
    k_l = axial_rope(k_l, cos, sin)
    sink_b = sink.astype(f32).reshape(N_KV_HEADS, GQA_GROUP)[:, :, None, None]

    qb = q_l.reshape(bsz, nb, ATTN_BLOCK, N_KV_HEADS, GQA_GROUP, HEAD_DIM)
    pad = ((0, 0), (ATTN_BLOCK, ATTN_BLOCK), (0, 0), (0, 0))
    kp = jnp.pad(k_l, pad).reshape(bsz, nb + 2, ATTN_BLOCK, N_KV_HEADS, HEAD_DIM)
    vp = jnp.pad(v_l, pad).reshape(bsz, nb + 2, ATTN_BLOCK, N_KV_HEADS, HEAD_DIM)
    k_win = jnp.concatenate([kp[:, o:o + nb] for o in range(3)], axis=2)
    v_win = jnp.concatenate([vp[:, o:o + nb] for o in range(3)], axis=2)
    s_loc = jnp.einsum('bnqkgd,bnjkd->bnkgqj', qb, k_win).astype(f32) * scale
    s_ctx = jnp.einsum('bnqkgd,bckd->bnkgqc', qb, k_c).astype(f32) * scale
    blk = jnp.arange(nb)[:, None, None]
    qpos = blk * ATTN_BLOCK + jnp.arange(ATTN_BLOCK)[None, :, None]
    kpos = (blk - 1) * ATTN_BLOCK + jnp.arange(3 * ATTN_BLOCK)[None, None, :]
    valid = (jnp.abs(kpos - qpos) <= WINDOW) & (kpos >= 0) & (kpos < S)
    s_loc = jnp.where(valid[None, :, None, None], s_loc, -jnp.inf)
    sink_l = jnp.broadcast_to(sink_b, s_loc.shape[:-1] + (1,))
    p = jax.nn.softmax(jnp.concatenate([s_ctx, s_loc, sink_l], axis=-1), axis=-1).astype(v_l.dtype)
    p_ctx = p[..., :n_ctx]
    p_loc = p[..., n_ctx:n_ctx + 3 * ATTN_BLOCK]
    o = (jnp.einsum('bnkgqc,bckd->bnqkgd', p_ctx, v_c)
         + jnp.einsum('bnkgqj,bnjkd->bnqkgd', p_loc, v_win))
    out_l = o.reshape(bsz, S, qd) @ out_w

    out_c = None
    if need_ctx_out:
        s_cc = jnp.einsum('bqkgd,bckd->bkgqc', q_c, k_c).astype(f32) * scale
        sink_c = jnp.broadcast_to(sink_b, s_cc.shape[:-1] + (1,))
        p_c = jax.nn.softmax(jnp.concatenate([s_cc, sink_c], axis=-1), axis=-1)[..., :n_ctx]
        o_c = jnp.einsum('bkgqc,bckd->bqkgd', p_c.astype(v_c.dtype), v_c)
        out_c = o_c.reshape(bsz, n_ctx, qd) @ out_w
    return out_l, out_c


def swiglu(t, wg, wu, wd):
    return (jax.nn.silu(t @ wg) * (t @ wu)) @ wd


def grouped_moe(h, router_w, router_bias, w_gate, w_up, w_down, sh_gate, sh_up, sh_down):
    shp = h.shape
    t = h.reshape(-1, shp[-1])
    n_tok = t.shape[0]
    scores = jax.nn.sigmoid((t @ router_w).astype(jnp.float32))
    biased = (scores + router_bias.astype(jnp.float32)).reshape(n_tok, N_EXPERT_GROUPS, EXPERTS_PER_GROUP)
    group_score = lax.top_k(biased, TOP_K)[0].sum(-1)
    _, g_idx = lax.top_k(group_score, TOPK_GROUPS)
    g_mask = jax.nn.one_hot(g_idx, N_EXPERT_GROUPS, dtype=jnp.float32).sum(1) > 0
    cand = jnp.where(g_mask[:, :, None], biased, -jnp.inf).reshape(n_tok, N_EXPERTS)
    _, e_idx = lax.top_k(cand, TOP_K)
    w = jnp.take_along_axis(scores, e_idx, axis=-1)
    w = w / w.sum(-1, keepdims=True)
    combine = (jax.nn.one_hot(e_idx, N_EXPERTS, dtype=jnp.float32) * w[..., None]).sum(1).astype(t.dtype)
    out = swiglu(t, sh_gate, sh_up, sh_down)
    for e in range(N_EXPERTS):
        out = out + combine[:, e:e + 1] * swiglu(t, w_gate[e], w_up[e], w_down[e])
    return out.reshape(shp)


def setup_inputs(seed: int = 0) -> dict:
    key = jax.random.key(seed)
    ks = jax.random.split(key, 32)
    nrm = jax.random.normal
    D = D_MODEL
    dt0 = jnp.exp(jax.random.uniform(ks[10], (N_SSD_LAYERS, 2, SSM_HEADS),
                                     minval=np.log(1e-3), maxval=np.log(1e-1)))
    return {
        'x': nrm(ks[0], (BATCH, SEQ, D)),
        'c': nrm(ks[1], (BATCH, D)),
        'ctx': nrm(ks[2], (BATCH, CTX_LEN, D)),
        'c_ctx': nrm(ks[3], (D,)),
        'ada_w': nrm(ks[4], (DEPTH, D, 6 * D)) * (0.5 * D ** -0.5),
        'ada_b': nrm(ks[5], (DEPTH, 6 * D)) * 0.02,
        'norm1_w': 1.0 + 0.05 * nrm(ks[6], (DEPTH, D)),
        'norm2_w': 1.0 + 0.05 * nrm(ks[7], (DEPTH, D)),
        'ssd_in_w': nrm(ks[8], (N_SSD_LAYERS, D, D_IN_PROJ)) * D ** -0.5,
        'ssd_conv_w': nrm(ks[9], (N_SSD_LAYERS, SSM_CONV, CONV_DIM)) * SSM_CONV ** -0.5,
        'ssd_conv_b': nrm(ks[11], (N_SSD_LAYERS, CONV_DIM)) * 0.02,
        'ssd_dt_bias': dt0 + jnp.log(-jnp.expm1(-dt0)),
        'ssd_a_log': jnp.log(jax.random.uniform(ks[12], (N_SSD_LAYERS, 2, SSM_HEADS), minval=1.0, maxval=16.0)),
        'ssd_d': 1.0 + 0.1 * nrm(ks[13], (N_SSD_LAYERS, SSM_HEADS)),
        'ssd_norm_w': 1.0 + 0.05 * nrm(ks[14], (N_SSD_LAYERS, D_INNER)),
        'ssd_out_w': nrm(ks[15], (N_SSD_LAYERS, D_INNER, D)) * D_INNER ** -0.5,
        'attn_qkv_w': nrm(ks[16], (N_ATTN_LAYERS, D, QKV_DIM)) * D ** -0.5,
        'attn_sink': 0.5 * nrm(ks[17], (N_ATTN_LAYERS, N_Q_HEADS)),
        'attn_out_w': nrm(ks[18], (N_ATTN_LAYERS, N_Q_HEADS * HEAD_DIM, D)) * (N_Q_HEADS * HEAD_DIM) ** -0.5,
        'router_w': nrm(ks[19], (D, N_EXPERTS)) * D ** -0.5,
        'router_bias': 0.01 * nrm(ks[20], (N_EXPERTS,)),
        'moe_w_gate': nrm(ks[21], (DEPTH, N_EXPERTS, D, D_EXPERT)) * D ** -0.5,
        'moe_w_up': nrm(ks[22], (DEPTH, N_EXPERTS, D, D_EXPERT)) * D ** -0.5,
        'moe_w_down': nrm(ks[23], (DEPTH, N_EXPERTS, D_EXPERT, D)) * D_EXPERT ** -0.5,
        'shared_w_gate': nrm(ks[24], (DEPTH, D, D_SHARED)) * D ** -0.5,
        'shared_w_up': nrm(ks[25], (DEPTH, D, D_SHARED)) * D ** -0.5,
        'shared_w_down': nrm(ks[26], (DEPTH, D_SHARED, D)) * D_SHARED ** -0.5,
        'final_norm_w': 1.0 + 0.05 * nrm(ks[27], (D,)),
    }


def reference(x, c, ctx, c_ctx, ada_w, ada_b, norm1_w, norm2_w, ssd_in_w, ssd_conv_w, ssd_conv_b,
              ssd_dt_bias, ssd_a_log, ssd_d, ssd_norm_w, ssd_out_w, attn_qkv_w, attn_sink, attn_out_w,
              router_w, router_bias, moe_w_gate, moe_w_up, moe_w_down, shared_w_gate, shared_w_up,
              shared_w_down, final_norm_w):
    rows = x.shape[1] // GRID_W
    cos, sin = axial_rope_tables(rows)
    silu_c = jax.nn.silu(c)
    silu_cc = jax.nn.silu(c_ctx)
    h, hc = x, ctx
    for l in range(DEPTH):
        need_ctx_out = l < DEPTH - 1
        j = l // N_MIXERS
        sh1, sc1, g1, sh2, sc2, g2 = jnp.split((silu_c @ ada_w[l] + ada_b[l])[:, None, :], 6, axis=-1)
        csh1, csc1, cg1, csh2, csc2, cg2 = jnp.split(silu_cc @ ada_w[l] + ada_b[l], 6)
        a_l = modulate(rmsnorm(h, norm1_w[l]), sh1, sc1)
        a_c = modulate(rmsnorm(hc, norm1_w[l]), csh1, csc1)
        if l % N_MIXERS == 0:
            out_l, out_c = ssd_mixer(a_l, a_c, ssd_in_w[j], ssd_conv_w[j], ssd_conv_b[j], ssd_dt_bias[j],
                                     ssd_a_log[j], ssd_d[j], ssd_norm_w[j], ssd_out_w[j], need_ctx_out)
        else:
            out_l, out_c = attn_mixer(a_l, a_c, attn_qkv_w[j], attn_sink[j], attn_out_w[j], cos, sin, need_ctx_out)
        h = h + g1 * out_l
        m_l = modulate(rmsnorm(h, norm2_w[l]), sh2, sc2)
        h = h + g2 * grouped_moe(m_l, router_w, router_bias, moe_w_gate[l], moe_w_up[l], moe_w_down[l],
                                 shared_w_gate[l], shared_w_up[l], shared_w_down[l])
        if need_ctx_out:
            hc = hc + cg1 * out_c
            m_c = modulate(rmsnorm(hc, norm2_w[l]), csh2, csc2)
            hc = hc + cg2 * grouped_moe(m_c, router_w, router_bias, moe_w_gate[l], moe_w_up[l], moe_w_down[l],
                                        shared_w_gate[l], shared_w_up[l], shared_w_down[l])
    return rmsnorm(h, final_norm_w)
```

```python
import functools

import jax
import jax.numpy as jnp
import numpy as np
from jax import lax
from jax.experimental import pallas as pl
from jax.experimental.pallas import tpu as pltpu

F32 = jnp.float32
BF16 = jnp.bfloat16

D_MODEL = 1024
DEPTH = 4
EPS = 1e-6
GRID_W = 64

D_INNER = 2048
SSM_HEADDIM = 64
SSM_HEADS = 32
SSM_STATE = 128
SSM_GROUPS = 4
SSM_CONV = 5
GN = SSM_GROUPS * SSM_STATE
CONV_DIM = D_INNER + 2 * GN
CHUNK = 128

HEAD_DIM = 64
N_Q_HEADS = 16
N_KV_HEADS = 4
ROPE_BASE = 10000.0
ROPE_PAIRS = 16
WINDOW = 128

N_EXPERTS = 16
N_EXPERT_GROUPS = 4
EXPERTS_PER_GROUP = 4
D_EXPERT = 512

TILE = 256
LANES = 128
MOD_ROWS = 16
NEG = -1e30
VMEM_LIMIT = 56 * 1024 * 1024


def _cp(n_axes, sem="arbitrary"):
    return pltpu.CompilerParams(dimension_semantics=(sem,) * n_axes, vmem_limit_bytes=VMEM_LIMIT)


def _dot(a, b):
    return jnp.dot(a, b, preferred_element_type=F32)


def _dot_nt(a, b):
    return lax.dot_general(a, b, (((1,), (1,)), ((), ())), preferred_element_type=F32)


def _split_bf16(v):
    hi = v.astype(BF16)
    lo = (v - hi.astype(F32)).astype(BF16)
    return hi, lo


def _sigmoid(x):
    return 1.0 / (1.0 + jnp.exp(-x))


def _silu(x):
    return x * _sigmoid(x)


def _softplus(x):
    return jnp.maximum(x, 0.0) + jnp.log(1.0 + jnp.exp(-jnp.abs(x)))


def _rms_mod(x, nw, shift, scale):
    y = x * lax.rsqrt(jnp.mean(x * x, axis=-1, keepdims=True) + EPS)
    return (y * nw) * (1.0 + scale) + shift


def _ada_kernel(c_ref, w_ref, b_ref, o_ref):
    s = _silu(c_ref[...])
    o_ref[...] = jnp.dot(s, w_ref[...], preferred_element_type=F32,
                         precision=lax.Precision.HIGHEST) + b_ref[...]


def _ada(cc, ada_w, ada_b):
    depth, d, n = ada_w.shape
    tn = 1024
    return pl.pallas_call(
        _ada_kernel,
        grid=(depth, n // tn),
        in_specs=[
            pl.BlockSpec((MOD_ROWS, d), lambda l, j: (0, 0)),
            pl.BlockSpec((None, d, tn), lambda l, j: (l, 0, j)),
            pl.BlockSpec((None, 1, tn), lambda l, j: (l, 0, j)),
        ],
        out_specs=pl.BlockSpec((None, MOD_ROWS, tn), lambda l, j: (l, 0, j)),
        out_shape=jax.ShapeDtypeStruct((depth, MOD_ROWS, n), F32),
        compiler_params=_cp(2),
    )(cc, ada_w, ada_b.reshape(depth, 1, n))


def _rope128(x, cos, sin_signed, low_half):
    partner = jnp.where(low_half, pltpu.roll(x, LANES - ROPE_PAIRS, axis=1), pltpu.roll(x, ROPE_PAIRS, axis=1))
    return x * cos + partner * sin_signed


def _nmm_kernel(*refs, n_out, rope, scales):
    x_ref, nw_ref, sh_ref, sc_ref = refs[:4]
    w_refs = refs[4:4 + n_out]
    pos = 4 + n_out
    if any(rope):
        cos_ref, sin_ref = refs[pos:pos + 2]
        pos += 2
    o_refs = refs[pos:pos + n_out]
    a = _rms_mod(x_ref[...], nw_ref[...], sh_ref[...], sc_ref[...]).astype(BF16)
    if any(rope):
        lane = lax.broadcasted_iota(jnp.int32, (TILE, LANES), 1)
        low_half = (lane % (2 * ROPE_PAIRS)) < ROPE_PAIRS
        cos = cos_ref[...]
        sin = sin_ref[...]
    for k in range(n_out):
        if rope[k]:
            n = o_refs[k].shape[1]
            for j in range(n // LANES):
                acc = _dot(a, w_refs[k][:, j * LANES:(j + 1) * LANES])
                acc = _rope128(acc, cos, sin, low_half) * scales[k]
                o_refs[k][:, j * LANES:(j + 1) * LANES] = acc.astype(o_refs[k].dtype)
        else:
            o_refs[k][...] = (_dot(a, w_refs[k][...]) * scales[k]).astype(o_refs[k].dtype)


def _norm_mod_matmul(x, nw, mod, part, weights, out_dtypes, geo, rope=None, scales=None, tables=None):
    nt, d = x.shape
    n_out = len(weights)
    rope = tuple(rope) if rope is not None else (False,) * n_out
    scales = tuple(scales) if scales is not None else (1.0,) * n_out
    row = geo["mod_row"]
    in_specs = [
        pl.BlockSpec((TILE, d), lambda i: (i, 0)),
        pl.BlockSpec((1, d), lambda i: (0, 0)),
        pl.BlockSpec((None, 1, d), lambda i: (row(i), 0, part)),
        pl.BlockSpec((None, 1, d), lambda i: (row(i), 0, part + 1)),
    ]
    args = [x, nw.reshape(1, d), mod, mod]
    for w in weights:
        in_specs.append(pl.BlockSpec(w.shape, lambda i: (0, 0)))
        args.append(w)
    if any(rope):
        tpb = geo["tpb"]
        in_specs += [pl.BlockSpec((TILE, LANES), lambda i: (i % tpb, 0))] * 2
        args += list(tables)
    return pl.pallas_call(
        functools.partial(_nmm_kernel, n_out=n_out, rope=rope, scales=scales),
        grid=(nt // TILE,),
        in_specs=in_specs,
        out_specs=[pl.BlockSpec((TILE, w.shape[1]), lambda i: (i, 0)) for w in weights],
        out_shape=[jax.ShapeDtypeStruct((nt, w.shape[1]), dt) for w, dt in zip(weights, out_dtypes)],
        compiler_params=_cp(1, "parallel"),
    )(*args)


def _mmres_kernel(a_ref, w_ref, h_ref, g_ref, o_ref):
    o_ref[...] = h_ref[...] + g_ref[...] * _dot(a_ref[...], w_ref[...])


def _matmul_residual(a, w, h, mod, part, geo):
    nt, k = a.shape
    d = w.shape[1]
    row = geo["mod_row"]
    return pl.pallas_call(
        _mmres_kernel,
        grid=(nt // TILE,),
        in_specs=[
            pl.BlockSpec((TILE, k), lambda i: (i, 0)),
            pl.BlockSpec((k, d), lambda i: (0, 0)),
            pl.BlockSpec((TILE, d), lambda i: (i, 0)),
            pl.BlockSpec((None, 1, d), lambda i: (row(i), 0, part)),
        ],
        out_specs=pl.BlockSpec((TILE, d), lambda i: (i, 0)),
        out_shape=jax.ShapeDtypeStruct((nt, d), F32),
        compiler_params=_cp(1, "parallel"),
    )(a, w, h, mod)


def _ssd_chunk(x, bm, cm, dt, a_row, e_ref, state_ref, y_ref, direction):
    q = CHUNK
    ii = lax.broadcasted_iota(jnp.int32, (q, q), 0)
    jj = lax.broadcasted_iota(jnp.int32, (q, q), 1)
    tri = (jj <= ii) if direction == 0 else (jj >= ii)
    tri_b = jnp.where(tri, 1.0, 0.0).astype(BF16)
    da = dt * a_row
    da_hi, da_lo = _split_bf16(da)
    cum = _dot(tri_b, da_hi) + _dot(tri_b, da_lo)
    cum_t = cum.T
    e = e_ref[...]

    def expand(v):
        hi, lo = _split_bf16(v)
        return _dot(hi, e) + _dot(lo, e)

    dt_x = expand(dt)
    cum_x = expand(cum)
    edge = q - 1 if direction == 0 else 0
    tot_x = cum_x[edge:edge + 1, :]
    dtx = x * dt_x
    dtx_b = dtx.astype(BF16)
    to_end = (dtx * jnp.exp(tot_x - cum_x)).astype(BF16)
    from_start = jnp.exp(cum_x)
    chunk_decay = jnp.exp(tot_x)
    lane = lax.broadcasted_iota(jnp.int32, (q, LANES), 1)
    first_head = lane < SSM_HEADDIM
    gw = D_INNER // SSM_GROUPS
    for g in range(SSM_GROUPS):
        bg = bm[:, g * SSM_STATE:(g + 1) * SSM_STATE]
        cg = cm[:, g * SSM_STATE:(g + 1) * SSM_STATE]
        cb = _dot_nt(cg, bg)
        h_prev = state_ref[g]
        y_off = _dot(cg, h_prev.astype(BF16)) * from_start[:, g * gw:(g + 1) * gw]
        bg_t = bg.astype(F32).T.astype(BF16)
        state_ref[g] = h_prev * chunk_decay[:, g * gw:(g + 1) * gw] + _dot(bg_t, to_end[:, g * gw:(g + 1) * gw])
        for p in range(gw // LANES):
            ms = []
            for k in range(2):
                hl = 32 * direction + g * 8 + 2 * p + k
                seg = cum[:, hl:hl + 1] - cum_t[hl:hl + 1, :]
                decay = jnp.exp(jnp.where(tri, seg, NEG))
                ms.append((cb * decay).astype(BF16))
            lo = g * gw + p * LANES
            out2 = _dot(jnp.concatenate(ms, axis=0), dtx_b[:, lo:lo + LANES])
            y_diag = jnp.where(first_head, out2[:q], out2[q:])
            y_ref[:, lo:lo + LANES] = y_diag + y_off[:, p * LANES:(p + 1) * LANES]


def _ssd_bwd_kernel(cur_ref, prev_ref, next_ref, dt_ref, cw_ref, cbias_ref, dtb_ref, alog_ref, e_ref,
                    act_ref, yb_ref, ext_ref, state_ref, *, ncc, nch):
    s = pl.program_id(1)
    c = jnp.where(s < ncc, ncc - 1 - s, nch - 1 - (s - ncc))

    @pl.when(s == 0)
    def _():
        state_ref[...] = jnp.zeros_like(state_ref)

    has_prev = jnp.logical_and(c != 0, c != ncc)
    has_next = jnp.logical_and(c != ncc - 1, c != nch - 1)
    halo = prev_ref.shape[0]
    ext_ref[0:halo, :] = jnp.where(has_prev, prev_ref[...].astype(F32), 0.0)
    ext_ref[halo:halo + CHUNK, :] = cur_ref[...].astype(F32)
    ext_ref[halo + CHUNK:2 * halo + CHUNK, :] = jnp.where(has_next, next_ref[...].astype(F32), 0.0)
    pad = (SSM_CONV - 1) // 2
    acc = cbias_ref[...] + cw_ref[0:1, :] * ext_ref[pl.ds(halo - pad, CHUNK), :]
    for k in range(1, SSM_CONV):
        acc = acc + cw_ref[k:k + 1, :] * ext_ref[pl.ds(halo - pad + k, CHUNK), :]
    act = _silu(acc)
    act_b = act.astype(BF16)
    act_ref[...] = act_b

    dt = _softplus(dt_ref[...] + dtb_ref[...])
    a_row = -jnp.exp(alog_ref[...])
    _ssd_chunk(act[:, :D_INNER], act_b[:, D_INNER:D_INNER + GN], act_b[:, D_INNER + GN:], dt, a_row,
               e_ref, state_ref, yb_ref, 1)


def _ssd_fwd_kernel(act_ref, dt_ref, yb_ref, z_ref, dtb_ref, alog_ref, dskip_ref, nw_ref, e_ref,
                    o_ref, yf_ref, state_ref):
    s = pl.program_id(1)

    @pl.when(s == 0)
    def _():
        state_ref[...] = jnp.zeros_like(state_ref)

    act_b = act_ref[...]
    x = act_b[:, :D_INNER].astype(F32)
    dt = _softplus(dt_ref[...] + dtb_ref[...])
    a_row = -jnp.exp(alog_ref[...])
    _ssd_chunk(x, act_b[:, D_INNER:D_INNER + GN], act_b[:, D_INNER + GN:], dt, a_row, e_ref, state_ref, yf_ref, 0)
    y = yf_ref[...] + yb_ref[...] + x * dskip_ref[...]
    y = y * _silu(z_ref[...].astype(F32))
    y = y * lax.rsqrt(jnp.mean(y * y, axis=-1, keepdims=True) + EPS)
    o_ref[...] = (y * nw_ref[...]).astype(BF16)


def _ssd_mixer(z, xbc, dt_raw, conv_w, conv_b, dt_bias, a_log, d_skip, norm_w, geo):
    nt = z.shape[0]
    bsz, nch, ncc = geo["batch"], geo["nch"], geo["ncc"]
    halo = 16
    hb = CHUNK // halo
    n_halo_blocks = nt // halo

    def chunk_bwd(b, s):
        return b * nch + jnp.where(s < ncc, ncc - 1 - s, nch - 1 - (s - ncc))

    def chunk_fwd(b, s):
        return b * nch + s

    cw = jnp.zeros((8, CONV_DIM), F32).at[:SSM_CONV].set(conv_w)
    dtb = jnp.zeros((1, LANES), F32).at[0, :2 * SSM_HEADS].set(dt_bias.reshape(-1))
    alog = jnp.zeros((1, LANES), F32).at[0, :2 * SSM_HEADS].set(a_log.reshape(-1))
    heads = np.arange(D_INNER) // SSM_HEADDIM
    e_np = np.zeros((2, LANES, D_INNER), np.float32)
    for d in range(2):
        e_np[d, 32 * d + heads, np.arange(D_INNER)] = 1.0
    e_mats = jnp.asarray(e_np, BF16)
    const = lambda shape: pl.BlockSpec(shape, lambda b, s: (0,) * len(shape))

    act, yb = pl.pallas_call(
        functools.partial(_ssd_bwd_kernel, ncc=ncc, nch=nch),
        grid=(bsz, nch),
        in_specs=[
            pl.BlockSpec((CHUNK, CONV_DIM), lambda b, s: (chunk_bwd(b, s), 0)),
            pl.BlockSpec((halo, CONV_DIM), lambda b, s: (jnp.maximum(chunk_bwd(b, s) * hb - 1, 0), 0)),
            pl.BlockSpec((halo, CONV_DIM), lambda b, s: (jnp.minimum((chunk_bwd(b, s) + 1) * hb, n_halo_blocks - 1), 0)),
            pl.BlockSpec((CHUNK, LANES), lambda b, s: (chunk_bwd(b, s), 0)),
            const((8, CONV_DIM)),
            const((1, CONV_DIM)),
            const((1, LANES)),
            const((1, LANES)),
            pl.BlockSpec((None, LANES, D_INNER), lambda b, s: (1, 0, 0)),
        ],
        out_specs=[
            pl.BlockSpec((CHUNK, CONV_DIM), lambda b, s: (chunk_bwd(b, s), 0)),
            pl.BlockSpec((CHUNK, D_INNER), lambda b, s: (chunk_bwd(b, s), 0)),
        ],
        out_shape=[jax.ShapeDtypeStruct((nt, CONV_DIM), BF16), jax.ShapeDtypeStruct((nt, D_INNER), F32)],
        scratch_shapes=[
            pltpu.VMEM((CHUNK + 2 * halo, CONV_DIM), F32),
            pltpu.VMEM((SSM_GROUPS, SSM_STATE, D_INNER // SSM_GROUPS), F32),
        ],
        compiler_params=_cp(2),
    )(xbc, xbc, xbc, dt_raw, cw, conv_b.reshape(1, CONV_DIM), dtb, alog, e_mats)

    return pl.pallas_call(
        _ssd_fwd_kernel,
        grid=(bsz, nch),
        in_specs=[
            pl.BlockSpec((CHUNK, CONV_DIM), lambda b, s: (chunk_fwd(b, s), 0)),
            pl.BlockSpec((CHUNK, LANES), lambda b, s: (chunk_fwd(b, s), 0)),
            pl.BlockSpec((CHUNK, D_INNER), lambda b, s: (chunk_fwd(b, s), 0)),
            pl.BlockSpec((CHUNK, D_INNER), lambda b, s: (chunk_fwd(b, s), 0)),
            const((1, LANES)),
            const((1, LANES)),
            const((1, D_INNER)),
            const((1, D_INNER)),
            pl.BlockSpec((None, LANES, D_INNER), lambda b, s: (0, 0, 0)),
        ],
        out_specs=pl.BlockSpec((CHUNK, D_INNER), lambda b, s: (chunk_fwd(b, s), 0)),
        out_shape=jax.ShapeDtypeStruct((nt, D_INNER), BF16),
        scratch_shapes=[
            pltpu.VMEM((CHUNK, D_INNER), F32),
            pltpu.VMEM((SSM_GROUPS, SSM_STATE, D_INNER // SSM_GROUPS), F32),
        ],
        compiler_params=_cp(2),
    )(act, dt_raw, yb, z, dtb, alog, jnp.repeat(d_skip, SSM_HEADDIM).reshape(1, D_INNER),
      norm_w.reshape(1, D_INNER), e_mats)


def _attn_kernel(sink_ref, q_ref, kc_ref, vc_ref, kp_ref, ko_ref, kn_ref, vp_ref, vo_ref, vn_ref, o_ref,
                 *, ncc, n_lat_blocks):
    blk = pl.program_id(1)
    n = blk - ncc
    is_lat = n >= 0
    q = CHUNK
    ii = lax.broadcasted_iota(jnp.int32, (q, q), 0)
    jj = lax.broadcasted_iota(jnp.int32, (q, q), 1)
    ok_prev = jnp.logical_and(jj >= ii, jnp.logical_and(is_lat, n >= 1))
    ok_own = jnp.logical_and(jj >= 0, is_lat)
    ok_next = jnp.logical_and(jj <= ii, jnp.logical_and(is_lat, n <= n_lat_blocks - 2))
    n_ctx = kc_ref.shape[0]
    lane = lax.broadcasted_iota(jnp.int32, (q, LANES), 1)
    low = lane < HEAD_DIM
    zero = jnp.zeros((q, LANES), BF16)
    for kh in range(N_KV_HEADS):
        ks = slice(kh * LANES, (kh + 1) * LANES)
        kcat = jnp.concatenate([kc_ref[:, ks], kp_ref[:, ks], ko_ref[:, ks], kn_ref[:, ks]], axis=0)
        vcat = jnp.concatenate([vc_ref[:, ks], vp_ref[:, ks], vo_ref[:, ks], vn_ref[:, ks]], axis=0)
        lhs = []
        for m in range(2):
            qp = q_ref[:, (2 * kh + m) * LANES:(2 * kh + m + 1) * LANES]
            lhs += [jnp.where(low, qp, zero), jnp.where(low, zero, qp)]
        s_all = _dot_nt(jnp.concatenate(lhs, axis=0), kcat)
        ps, inv = [], []
        for gi in range(4):
            sink = sink_ref[kh * 4 + gi]
            sh = s_all[gi * q:(gi + 1) * q]
            sc = jnp.concatenate([sh[:, :n_ctx],
                                  jnp.where(ok_prev, sh[:, n_ctx:n_ctx + q], NEG),
                                  jnp.where(ok_own, sh[:, n_ctx + q:n_ctx + 2 * q], NEG),
                                  jnp.where(ok_next, sh[:, n_ctx + 2 * q:], NEG)], axis=1)
            mx = jnp.maximum(jnp.max(sc, axis=-1, keepdims=True), sink)
            p = jnp.exp(sc - mx)
            denom = jnp.sum(p, axis=-1, keepdims=True) + jnp.exp(sink - mx)
            ps.append(p.astype(BF16))
            inv.append(1.0 / denom)
        r = _dot(jnp.concatenate(ps, axis=0), vcat)
        for m in range(2):
            o = jnp.where(low, r[(2 * m) * q:(2 * m + 1) * q] * inv[2 * m],
                          r[(2 * m + 1) * q:(2 * m + 2) * q] * inv[2 * m + 1])
            o_ref[:, (2 * kh + m) * LANES:(2 * kh + m + 1) * LANES] = o.astype(BF16)


def _attention(qr, kd, vd, sink, geo):
    nt = qr.shape[0]
    bsz, nch, ncc = geo["batch"], geo["nch"], geo["ncc"]
    nlb = nch - ncc
    kvw = kd.shape[1]
    ctx_rows = ncc * CHUNK

    def win(o):
        return lambda b, j, *_: (b * nch + ncc + jnp.clip(j - ncc + o - 1, 0, nlb - 1), 0)

    grid_spec = pltpu.PrefetchScalarGridSpec(
        num_scalar_prefetch=1,
        grid=(bsz, nch),
        in_specs=[
            pl.BlockSpec((CHUNK, qr.shape[1]), lambda b, j, *_: (b * nch + j, 0)),
            pl.BlockSpec((ctx_rows, kvw), lambda b, j, *_: (b * (nch // ncc), 0)),
            pl.BlockSpec((ctx_rows, kvw), lambda b, j, *_: (b * (nch // ncc), 0)),
            pl.BlockSpec((CHUNK, kvw), win(0)),
            pl.BlockSpec((CHUNK, kvw), win(1)),
            pl.BlockSpec((CHUNK, kvw), win(2)),
            pl.BlockSpec((CHUNK, kvw), win(0)),
            pl.BlockSpec((CHUNK, kvw), win(1)),
            pl.BlockSpec((CHUNK, kvw), win(2)),
        ],
        out_specs=pl.BlockSpec((CHUNK, qr.shape[1]), lambda b, j, *_: (b * nch + j, 0)),
    )
    return pl.pallas_call(
        functools.partial(_attn_kernel, ncc=ncc, n_lat_blocks=nlb),
        grid_spec=grid_spec,
        out_shape=jax.ShapeDtypeStruct((nt, qr.shape[1]), BF16),
        compiler_params=_cp(2, "parallel"),
    )(sink, qr, kd, vd, kd, kd, kd, vd, vd, vd)


def _route_kernel(h_ref, nw_ref, sh_ref, sc_ref, rwt_ref, rb_ref, wg_ref, wu_ref, wd_ref, xs_ref,
                  shared_ref, idx_ref, wcol_ref, cnt_ref,
                  m_ref, pos_v_ref, pos_s_ref, carry_ref, sem, *, cap):
    i = pl.program_id(0)
    t = TILE

    @pl.when(i == 0)
    def _():
        carry_ref[...] = jnp.zeros_like(carry_ref)

    m = _rms_mod(h_ref[...], nw_ref[...], sh_ref[...], sc_ref[...])
    m_ref[...] = m
    logits = lax.dot_general(rwt_ref[...], m, (((1,), (1,)), ((), ())), preferred_element_type=F32,
                             precision=lax.Precision.HIGHEST)
    scores = _sigmoid(logits)
    biased = scores + rb_ref[...]
    rows = [biased[e:e + 1, :] for e in range(N_EXPERTS)]
    srows = [scores[e:e + 1, :] for e in range(N_EXPERTS)]
    gscore = []
    for g in range(N_EXPERT_GROUPS):
        r = rows[g * 4:(g + 1) * 4]
        best = None
        for a in range(4):
            for b in range(a + 1, 4):
                pair = r[a] + r[b]
                best = pair if best is None else jnp.maximum(best, pair)
        gscore.append(best)
    gbest = jnp.maximum(jnp.maximum(gscore[0], gscore[1]), jnp.maximum(gscore[2], gscore[3]))
    gsel = jnp.full((1, t), N_EXPERT_GROUPS - 1, jnp.int32)
    for g in range(N_EXPERT_GROUPS - 2, -1, -1):
        gsel = jnp.where(gscore[g] == gbest, g, gsel)
    cand = [jnp.where(gsel == e // 4, rows[e], NEG) for e in range(N_EXPERTS)]
    best1 = functools.reduce(jnp.maximum, cand)
    e1 = jnp.full((1, t), N_EXPERTS - 1, jnp.int32)
    for e in range(N_EXPERTS - 2, -1, -1):
        e1 = jnp.where(cand[e] == best1, e, e1)
    cand2 = [jnp.where(e1 == e, NEG, cand[e]) for e in range(N_EXPERTS)]
    best2 = functools.reduce(jnp.maximum, cand2)
    e2 = jnp.full((1, t), N_EXPERTS - 1, jnp.int32)
    for e in range(N_EXPERTS - 2, -1, -1):
        e2 = jnp.where(cand2[e] == best2, e, e2)
    s1 = functools.reduce(jnp.add, [jnp.where(e1 == e, srows[e], 0.0) for e in range(N_EXPERTS)])
    s2 = functools.reduce(jnp.add, [jnp.where(e2 == e, srows[e], 0.0) for e in range(N_EXPERTS)])
    wsum = s1 + s2
    w1 = s1 / wsum
    w2 = s2 / wsum
    eid = lax.broadcasted_iota(jnp.int32, (N_EXPERTS, t), 0)
    oh1 = eid == e1
    oh2 = eid == e2
    onehot = jnp.where(jnp.logical_or(oh1, oh2), 1.0, 0.0)
    jr = lax.broadcasted_iota(jnp.int32, (t, t), 0)
    jc = lax.broadcasted_iota(jnp.int32, (t, t), 1)
    before = jnp.where(jr < jc, 1.0, 0.0).astype(BF16)
    prefix = _dot(onehot.astype(BF16), before) + carry_ref[:, 0:1]
    carry_ref[...] = carry_ref[...] + jnp.sum(onehot, axis=1, keepdims=True)
    r1 = jnp.sum(jnp.where(oh1, prefix, 0.0), axis=0, keepdims=True).astype(jnp.int32)
    r2 = jnp.sum(jnp.where(oh2, prefix, 0.0), axis=0, keepdims=True).astype(jnp.int32)
    pos1 = e1 * cap + r1
    pos2 = e2 * cap + r2
    idx = jnp.concatenate([e1, e2, pos1, pos2, jnp.zeros((4, t), jnp.int32)], axis=0)
    idx_ref[...] = idx
    pos_v_ref[...] = idx
    wrows = jnp.concatenate([w1, w2, jnp.zeros((LANES - 2, t), F32)], axis=0)
    wcol_ref[...] = wrows.T
    cnt_ref[...] = carry_ref[...].astype(jnp.int32)

    cp = pltpu.make_async_copy(pos_v_ref, pos_s_ref, sem.at[1])
    cp.start()
    cp.wait()

    def row_copy(r, k):
        return pltpu.make_async_copy(m_ref.at[pl.ds(r, 1)], xs_ref.at[pl.ds(pos_s_ref[2 + k, r], 1)], sem.at[0])

    def issue(r, carry):
        row_copy(r, 0).start()
        row_copy(r, 1).start()
        return carry

    lax.fori_loop(0, t, issue, 0)

    mb = m.astype(BF16)
    hid = (_silu(_dot(mb, wg_ref[...])) * _dot(mb, wu_ref[...])).astype(BF16)
    shared_ref[...] = _dot(hid, wd_ref[...])

    def drain(r, carry):
        row_copy(r, 0).wait()
        row_copy(r, 1).wait()
        return carry

    lax.fori_loop(0, t, drain, 0)


def _expert_kernel(blk_ref, exp_ref, nv_ref, x_ref, wg_ref, wu_ref, wd_ref, y_ref):
    j = pl.program_id(0)
    nv = nv_ref[j]

    @pl.when(nv > 0)
    def _():
        rows = lax.broadcasted_iota(jnp.int32, (TILE, 1), 0)
        x = jnp.where(rows < nv, x_ref[...], 0.0).astype(BF16)
        hid = (_silu(_dot(x, wg_ref[...])) * _dot(x, wu_ref[...])).astype(BF16)
        y_ref[...] = _dot(hid, wd_ref[...])

    @pl.when(nv <= 0)
    def _():
        y_ref[...] = jnp.zeros_like(y_ref)


def _combine_kernel(h_ref, shared_ref, g_ref, idx_ref, wcol_ref, ys_ref, o_ref, buf_ref, pos_s_ref, sem):
    t = TILE
    cp = pltpu.make_async_copy(idx_ref, pos_s_ref, sem.at[1])
    cp.start()
    cp.wait()

    def row_copy(r, k):
        return pltpu.make_async_copy(ys_ref.at[pl.ds(pos_s_ref[2 + k, r], 1)], buf_ref.at[k, pl.ds(r, 1)], sem.at[0])

    def issue(r, carry):
        row_copy(r, 0).start()
        row_copy(r, 1).start()
        return carry

    lax.fori_loop(0, t, issue, 0)

    def drain(r, carry):
        row_copy(r, 0).wait()
        row_copy(r, 1).wait()
        return carry

    lax.fori_loop(0, t, drain, 0)
    w = wcol_ref[...]
    routed = w[:, 0:1] * buf_ref[0] + w[:, 1:2] * buf_ref[1]
    o_ref[...] = h_ref[...] + g_ref[...] * (shared_ref[...] + routed)


def _moe(h, nw, mod, router_wt, router_b, wg, wu, wd, swg, swu, swd, geo):
    nt, d = h.shape
    ntiles = nt // TILE
    row = geo["mod_row"]
    cap = nt
    dump_blk = N_EXPERTS * cap // TILE
    n_rows = N_EXPERTS * cap + TILE
    const = lambda shape: pl.BlockSpec(shape, lambda i: (0,) * len(shape))

    xs, shared, idx, wcol, cnt = pl.pallas_call(
        functools.partial(_route_kernel, cap=cap),
        grid=(ntiles,),
        in_specs=[
            pl.BlockSpec((TILE, d), lambda i: (i, 0)),
            const((1, d)),
            pl.BlockSpec((None, 1, d), lambda i: (row(i), 0, 3)),
            pl.BlockSpec((None, 1, d), lambda i: (row(i), 0, 4)),
            const((N_EXPERTS, d)),
            const((N_EXPERTS, 1)),
            const(swg.shape), const(swu.shape), const(swd.shape),
        ],
        out_specs=[
            pl.BlockSpec(memory_space=pl.ANY),
            pl.BlockSpec((TILE, d), lambda i: (i, 0)),
            pl.BlockSpec((8, TILE), lambda i: (0, i)),
            pl.BlockSpec((TILE, LANES), lambda i: (i, 0)),
            const((N_EXPERTS, LANES)),
        ],
        out_shape=[
            jax.ShapeDtypeStruct((n_rows, d), F32),
            jax.ShapeDtypeStruct((nt, d), F32),
            jax.ShapeDtypeStruct((8, nt), jnp.int32),
            jax.ShapeDtypeStruct((nt, LANES), F32),
            jax.ShapeDtypeStruct((N_EXPERTS, LANES), jnp.int32),
        ],
        scratch_shapes=[
            pltpu.VMEM((TILE, d), F32),
            pltpu.VMEM((8, TILE), jnp.int32),
            pltpu.SMEM((8, TILE), jnp.int32),
            pltpu.VMEM((N_EXPERTS, LANES), F32),
            pltpu.SemaphoreType.DMA((2,)),
        ],
        compiler_params=_cp(1),
    )(h, nw.reshape(1, d), mod, mod, router_wt, router_b.reshape(N_EXPERTS, 1), swg, swu, swd)

    counts = cnt[:, 0]
    tiles_e = (counts + TILE - 1) // TILE
    ends = jnp.cumsum(tiles_e)
    starts = ends - tiles_e
    n_sched = 2 * ntiles + N_EXPERTS
    jidx = jnp.arange(n_sched, dtype=jnp.int32)
    e_of = jnp.minimum(jnp.searchsorted(ends, jidx, side="right"), N_EXPERTS - 1).astype(jnp.int32)
    local = jidx - starts[e_of]
    active = jidx < ends[-1]
    last_e = e_of[jnp.maximum(ends[-1] - 1, 0)]
    tile_blk = jnp.where(active, e_of * (cap // TILE) + local, dump_blk).astype(jnp.int32)
    tile_e = jnp.where(active, e_of, last_e).astype(jnp.int32)
    tile_nv = jnp.where(active, jnp.minimum(counts[e_of] - local * TILE, TILE), 0).astype(jnp.int32)

    ys = pl.pallas_call(
        _expert_kernel,
        grid_spec=pltpu.PrefetchScalarGridSpec(
            num_scalar_prefetch=3,
            grid=(n_sched,),
            in_specs=[
                pl.BlockSpec((TILE, d), lambda j, blk, ex, nv: (blk[j], 0)),
                pl.BlockSpec((None, d, D_EXPERT), lambda j, blk, ex, nv: (ex[j], 0, 0)),
                pl.BlockSpec((None, d, D_EXPERT), lambda j, blk, ex, nv: (ex[j], 0, 0)),
                pl.BlockSpec((None, D_EXPERT, d), lambda j, blk, ex, nv: (ex[j], 0, 0)),
            ],
            out_specs=pl.BlockSpec((TILE, d), lambda j, blk, ex, nv: (blk[j], 0)),
        ),
        out_shape=jax.ShapeDtypeStruct((n_rows, d), F32),
        compiler_params=_cp(1),
    )(tile_blk, tile_e, tile_nv, xs, wg, wu, wd)

    return pl.pallas_call(
        _combine_kernel,
        grid=(ntiles,),
        in_specs=[
            pl.BlockSpec((TILE, d), lambda i: (i, 0)),
            pl.BlockSpec((TILE, d), lambda i: (i, 0)),
            pl.BlockSpec((None, 1, d), lambda i: (row(i), 0, 5)),
            pl.BlockSpec((8, TILE), lambda i: (0, i)),
            pl.BlockSpec((TILE, LANES), lambda i: (i, 0)),
            pl.BlockSpec(memory_space=pl.ANY),
        ],
        out_specs=pl.BlockSpec((TILE, d), lambda i: (i, 0)),
        out_shape=jax.ShapeDtypeStruct((nt, d), F32),
        scratch_shapes=[
            pltpu.VMEM((2, TILE, d), F32),
            pltpu.SMEM((8, TILE), jnp.int32),
            pltpu.SemaphoreType.DMA((2,)),
        ],
        compiler_params=_cp(1),
    )(h, shared, mod, idx, wcol, ys)


def _final_kernel(h_ref, w_ref, o_ref):
    x = h_ref[...]
    o_ref[...] = x * lax.rsqrt(jnp.mean(x * x, axis=-1, keepdims=True) + EPS) * w_ref[...]


def _final_norm(h, w, geo):
    d = h.shape[1]
    tpb, bsz = geo["tpb"], geo["batch"]
    lat = tpb - 1
    return pl.pallas_call(
        _final_kernel,
        grid=(bsz * lat,),
        in_specs=[
            pl.BlockSpec((TILE, d), lambda i: ((i // lat) * tpb + 1 + i % lat, 0)),
            pl.BlockSpec((1, d), lambda i: (0, 0)),
        ],
        out_specs=pl.BlockSpec((TILE, d), lambda i: (i, 0)),
        out_shape=jax.ShapeDtypeStruct((bsz * lat * TILE, d), F32),
        compiler_params=_cp(1, "parallel"),
    )(h, w.reshape(1, d))


def _rope_tables(n_ctx, seq):
    lane = np.arange(LANES) % HEAD_DIM
    axis = lane // (2 * ROPE_PAIRS)
    pair = lane % ROPE_PAIRS
    sign = np.where((lane % (2 * ROPE_PAIRS)) < ROPE_PAIRS, -1.0, 1.0).astype(np.float32)
    inv_freq = ROPE_BASE ** (-jnp.arange(ROPE_PAIRS, dtype=F32) / ROPE_PAIRS)
    t = jnp.arange(seq)
    posn = jnp.stack([t // GRID_W, t % GRID_W], axis=-1).astype(F32)
    ang = posn[:, axis] * inv_freq[pair][None, :]
    cos = jnp.concatenate([jnp.ones((n_ctx, LANES), F32), jnp.cos(ang)], axis=0)
    sin = jnp.concatenate([jnp.zeros((n_ctx, LANES), F32), jnp.sin(ang) * sign[None, :]], axis=0)
    return cos, sin


def kernel(x, c, ctx, c_ctx, ada_w, ada_b, norm1_w, norm2_w, ssd_in_w, ssd_conv_w, ssd_conv_b, ssd_dt_bias,
           ssd_a_log, ssd_d, ssd_norm_w, ssd_out_w, attn_qkv_w, attn_sink, attn_out_w, router_w, router_bias,
           moe_w_gate, moe_w_up, moe_w_down, shared_w_gate, shared_w_up, shared_w_down, final_norm_w):
    bsz, seq, d = x.shape
    n_ctx = ctx.shape[1]
    assert n_ctx == TILE and seq % TILE == 0 and d == D_MODEL and bsz < MOD_ROWS
    tpb = (n_ctx + seq) // TILE
    geo = {
        "batch": bsz,
        "tpb": tpb,
        "nch": (n_ctx + seq) // CHUNK,
        "ncc": n_ctx // CHUNK,
        "mod_row": lambda i: jnp.where(i % tpb == 0, bsz, i // tpb),
    }
    nt = bsz * (n_ctx + seq)
    h = jnp.concatenate([ctx, x], axis=1).reshape(nt, d)

    cc = jnp.zeros((MOD_ROWS, d), F32).at[:bsz].set(c).at[bsz].set(c_ctx)
    mod = _ada(cc, ada_w, ada_b).reshape(DEPTH, MOD_ROWS, 1, 6 * d)
    cos, sin = _rope_tables(n_ctx, seq)
    router_wt = router_w.T

    for l in range(DEPTH):
        j = l // 2
        if l % 2 == 0:
            w_in = ssd_in_w[j].astype(BF16)
            w_z = w_in[:, :D_INNER]
            w_xbc = w_in[:, D_INNER:D_INNER + CONV_DIM]
            w_dt = jnp.zeros((d, LANES), BF16).at[:, :2 * SSM_HEADS].set(w_in[:, D_INNER + CONV_DIM:])
            z, xbc, dt_raw = _norm_mod_matmul(h, norm1_w[l], mod[l], 0, [w_z, w_xbc, w_dt], [BF16, BF16, F32], geo)
            yn = _ssd_mixer(z, xbc, dt_raw, ssd_conv_w[j], ssd_conv_b[j], ssd_dt_bias[j], ssd_a_log[j],
                            ssd_d[j], ssd_norm_w[j], geo)
            h = _matmul_residual(yn, ssd_out_w[j].astype(BF16), h, mod[l], 2, geo)
        else:
            w_qkv = attn_qkv_w[j].astype(BF16)
            qd = N_Q_HEADS * HEAD_DIM
            kd = N_KV_HEADS * HEAD_DIM
            dup = lambda w: jnp.repeat(w.reshape(d, N_KV_HEADS, 1, HEAD_DIM), 2, axis=2).reshape(d, 2 * kd)
            w_q = w_qkv[:, :qd]
            w_k = dup(w_qkv[:, qd:qd + kd])
            w_v = dup(w_qkv[:, qd + kd:])
            qr, kdup, vdup = _norm_mod_matmul(h, norm1_w[l], mod[l], 0, [w_q, w_k, w_v], [BF16, BF16, BF16], geo,
                                              rope=(True, True, False), scales=(HEAD_DIM ** -0.5, 1.0, 1.0),
                                              tables=(cos, sin))
            o = _attention(qr, kdup, vdup, attn_sink[j], geo)
            h = _matmul_residual(o, attn_out_w[j].astype(BF16), h, mod[l], 2, geo)
        h = _moe(h, norm2_w[l], mod[l], router_wt, router_bias,
                 moe_w_gate[l].astype(BF16), moe_w_up[l].astype(BF16), moe_w_down[l].astype(BF16),
                 shared_w_gate[l].astype(BF16), shared_w_up[l].astype(BF16), shared_w_down[l].astype(BF16), geo)

    return _final_norm(h, final_norm_w, geo).reshape(bsz, seq, d)
```

```python
import functools

import jax
import jax.numpy as jnp
import numpy as np
from jax import lax
from jax.experimental import pallas as pl
from jax.experimental.pallas import tpu as pltpu

F32 = jnp.float32
BF16 = jnp.bfloat16

D_MODEL = 1024
DEPTH = 4
EPS = 1e-6
GRID_W = 64

D_INNER = 2048
SSM_HEADDIM = 64
SSM_HEADS = 32
SSM_STATE = 128
SSM_GROUPS = 4
SSM_CONV = 5
GN = SSM_GROUPS * SSM_STATE
CONV_DIM = D_INNER + 2 * GN
CHUNK = 128

HEAD_DIM = 64
N_Q_HEADS = 16
N_KV_HEADS = 4
ROPE_BASE = 10000.0
ROPE_PAIRS = 16
WINDOW = 128

N_EXPERTS = 16
N_EXPERT_GROUPS = 4
EXPERTS_PER_GROUP = 4
D_EXPERT = 512

TILE = 256
LANES = 128
MOD_ROWS = 16
NEG = -1e30
VMEM_LIMIT = 56 * 1024 * 1024


def _cp(n_axes, sem="arbitrary"):
    return pltpu.CompilerParams(dimension_semantics=(sem,) * n_axes, vmem_limit_bytes=VMEM_LIMIT)


def _dot(a, b):
    return jnp.dot(a, b, preferred_element_type=F32)


def _dot_nt(a, b):
    return lax.dot_general(a, b, (((1,), (1,)), ((), ())), preferred_element_type=F32)


def _split_bf16(v):
    hi = v.astype(BF16)
    lo = (v - hi.astype(F32)).astype(BF16)
    return hi, lo


def _sigmoid(x):
    return 1.0 / (1.0 + jnp.exp(-x))


def _silu(x):
    return x * _sigmoid(x)


def _softplus(x):
    return jnp.maximum(x, 0.0) + jnp.log(1.0 + jnp.exp(-jnp.abs(x)))


def _rms_mod(x, nw, shift, scale):
    y = x * lax.rsqrt(jnp.mean(x * x, axis=-1, keepdims=True) + EPS)
    return (y * nw) * (1.0 + scale) + shift


def _ada_kernel(c_ref, w_ref, b_ref, o_ref):
    s = _silu(c_ref[...])
    o_ref[...] = jnp.dot(s, w_ref[...], preferred_element_type=F32,
                         precision=lax.Precision.HIGHEST) + b_ref[...]


def _ada(cc, ada_w, ada_b):
    depth, d, n = ada_w.shape
    tn = 1024
    return pl.pallas_call(
        _ada_kernel,
        grid=(depth, n // tn),
        in_specs=[
            pl.BlockSpec((MOD_ROWS, d), lambda l, j: (0, 0)),
            pl.BlockSpec((None, d, tn), lambda l, j: (l, 0, j)),
            pl.BlockSpec((None, 1, tn), lambda l, j: (l, 0, j)),
        ],
        out_specs=pl.BlockSpec((None, MOD_ROWS, tn), lambda l, j: (l, 0, j)),
        out_shape=jax.ShapeDtypeStruct((depth, MOD_ROWS, n), F32),
        compiler_params=_cp(2),
        name="ada_table",
    )(cc, ada_w, ada_b.reshape(depth, 1, n))


def _rope128(x, cos, sin_signed, low_half):
    partner = jnp.where(low_half, pltpu.roll(x, LANES - ROPE_PAIRS, axis=1), pltpu.roll(x, ROPE_PAIRS, axis=1))
    return x * cos + partner * sin_signed


def _nmm_kernel(*refs, n_out, rope, scales):
    x_ref, nw_ref, sh_ref, sc_ref = refs[:4]
    w_refs = refs[4:4 + n_out]
    pos = 4 + n_out
    if any(rope):
        cos_ref, sin_ref = refs[pos:pos + 2]
        pos += 2
    o_refs = refs[pos:pos + n_out]
    a = _rms_mod(x_ref[...], nw_ref[...], sh_ref[...], sc_ref[...]).astype(BF16)
    if any(rope):
        lane = lax.broadcasted_iota(jnp.int32, (TILE, LANES), 1)
        low_half = (lane % (2 * ROPE_PAIRS)) < ROPE_PAIRS
        cos = cos_ref[...]
        sin = sin_ref[...]
    for k in range(n_out):
        if rope[k]:
            n = o_refs[k].shape[1]
            for j in range(n // LANES):
                acc = _dot(a, w_refs[k][:, j * LANES:(j + 1) * LANES])
                acc = _rope128(acc, cos, sin, low_half) * scales[k]
                o_refs[k][:, j * LANES:(j + 1) * LANES] = acc.astype(o_refs[k].dtype)
        else:
            o_refs[k][...] = (_dot(a, w_refs[k][...]) * scales[k]).astype(o_refs[k].dtype)


def _norm_mod_matmul(x, nw, mod, part, weights, out_dtypes, geo, rope=None, scales=None, tables=None):
    nt, d = x.shape
    n_out = len(weights)
    rope = tuple(rope) if rope is not None else (False,) * n_out
    scales = tuple(scales) if scales is not None else (1.0,) * n_out
    row = geo["mod_row"]
    in_specs = [
        pl.BlockSpec((TILE, d), lambda i: (i, 0)),
        pl.BlockSpec((1, d), lambda i: (0, 0)),
        pl.BlockSpec((None, 1, d), lambda i: (row(i), 0, part)),
        pl.BlockSpec((None, 1, d), lambda i: (row(i), 0, part + 1)),
    ]
    args = [x, nw.reshape(1, d), mod, mod]
    for w in weights:
        in_specs.append(pl.BlockSpec(w.shape, lambda i: (0, 0)))
        args.append(w)
    if any(rope):
        tpb = geo["tpb"]
        in_specs += [pl.BlockSpec((TILE, LANES), lambda i: (i % tpb, 0))] * 2
        args += list(tables)
    return pl.pallas_call(
        functools.partial(_nmm_kernel, n_out=n_out, rope=rope, scales=scales),
        grid=(nt // TILE,),
        in_specs=in_specs,
        out_specs=[pl.BlockSpec((TILE, w.shape[1]), lambda i: (i, 0)) for w in weights],
        out_shape=[jax.ShapeDtypeStruct((nt, w.shape[1]), dt) for w, dt in zip(weights, out_dtypes)],
        compiler_params=_cp(1, "parallel"),
        name="norm_proj_rope" if any(rope) else "norm_proj",
    )(*args)


def _mmres_kernel(a_ref, w_ref, h_ref, g_ref, o_ref):
    o_ref[...] = h_ref[...] + g_ref[...] * _dot(a_ref[...], w_ref[...])


def _matmul_residual(a, w, h, mod, part, geo):
    nt, k = a.shape
    d = w.shape[1]
    row = geo["mod_row"]
    return pl.pallas_call(
        _mmres_kernel,
        grid=(nt // TILE,),
        in_specs=[
            pl.BlockSpec((TILE, k), lambda i: (i, 0)),
            pl.BlockSpec((k, d), lambda i: (0, 0)),
            pl.BlockSpec((TILE, d), lambda i: (i, 0)),
            pl.BlockSpec((None, 1, d), lambda i: (row(i), 0, part)),
        ],
        out_specs=pl.BlockSpec((TILE, d), lambda i: (i, 0)),
        out_shape=jax.ShapeDtypeStruct((nt, d), F32),
        compiler_params=_cp(1, "parallel"),
        name="proj_residual",
    )(a, w, h, mod)


def _ssd_chunk(x, bm, cm, dt, a_row, e_ref, state_ref, y_ref, direction):
    q = CHUNK
    ii = lax.broadcasted_iota(jnp.int32, (q, q), 0)
    jj = lax.broadcasted_iota(jnp.int32, (q, q), 1)
    tri = (jj <= ii) if direction == 0 else (jj >= ii)
    tri_b = jnp.where(tri, 1.0, 0.0).astype(BF16)
    da = dt * a_row
    da_hi, da_lo = _split_bf16(da)
    cum = _dot(tri_b, da_hi) + _dot(tri_b, da_lo)
    cum_t = cum.T
    e = e_ref[...]

    def expand(v):
        hi, lo = _split_bf16(v)
        return _dot(hi, e) + _dot(lo, e)

    dt_x = expand(dt)
    cum_x = expand(cum)
    edge = q - 1 if direction == 0 else 0
    tot_x = cum_x[edge:edge + 1, :]
    dtx = x * dt_x
    dtx_b = dtx.astype(BF16)
    to_end = (dtx * jnp.exp(tot_x - cum_x)).astype(BF16)
    from_start = jnp.exp(cum_x)
    chunk_decay = jnp.exp(tot_x)
    lane = lax.broadcasted_iota(jnp.int32, (q, LANES), 1)
    first_head = lane < SSM_HEADDIM
    gw = D_INNER // SSM_GROUPS
    for g in range(SSM_GROUPS):
        bg = bm[:, g * SSM_STATE:(g + 1) * SSM_STATE]
        cg = cm[:, g * SSM_STATE:(g + 1) * SSM_STATE]
        cb = _dot_nt(cg, bg)
        h_prev = state_ref[g]
        y_off = _dot(cg, h_prev.astype(BF16)) * from_start[:, g * gw:(g + 1) * gw]
        bg_t = bg.astype(F32).T.astype(BF16)
        state_ref[g] = h_prev * chunk_decay[:, g * gw:(g + 1) * gw] + _dot(bg_t, to_end[:, g * gw:(g + 1) * gw])
        for p in range(gw // LANES):
            ms = []
            for k in range(2):
                hl = 32 * direction + g * 8 + 2 * p + k
                seg = cum[:, hl:hl + 1] - cum_t[hl:hl + 1, :]
                decay = jnp.exp(jnp.where(tri, seg, NEG))
                ms.append((cb * decay).astype(BF16))
            lo = g * gw + p * LANES
            out2 = _dot(jnp.concatenate(ms, axis=0), dtx_b[:, lo:lo + LANES])
            y_diag = jnp.where(first_head, out2[:q], out2[q:])
            y_ref[:, lo:lo + LANES] = y_diag + y_off[:, p * LANES:(p + 1) * LANES]


def _ssd_bwd_kernel(cur_ref, prev_ref, next_ref, dt_ref, cw_ref, cbias_ref, dtb_ref, alog_ref, e_ref,
                    act_ref, yb_ref, state_ref, *, ncc, nch):
    s = pl.program_id(1)
    c = jnp.where(s < ncc, ncc - 1 - s, nch - 1 - (s - ncc))

    @pl.when(s == 0)
    def _():
        state_ref[...] = jnp.zeros_like(state_ref)

    has_prev = jnp.logical_and(c != 0, c != ncc)
    has_next = jnp.logical_and(c != ncc - 1, c != nch - 1)
    halo = prev_ref.shape[0]
    pad = (SSM_CONV - 1) // 2
    taps = [k for k in range(SSM_CONV) if k != pad]
    n_src = CHUNK + 2 * halo
    src = jnp.concatenate([prev_ref[...], cur_ref[...], next_ref[...]], axis=0)
    rr = lax.broadcasted_iota(jnp.int32, (len(taps) * CHUNK, n_src), 0)
    cc = lax.broadcasted_iota(jnp.int32, (len(taps) * CHUNK, n_src), 1)
    want = rr + (halo - pad)
    for n, k in enumerate(taps):
        want = jnp.where(rr >= n * CHUNK, rr - n * CHUNK + (halo - pad + k), want)
    inside = jnp.logical_and(jnp.logical_or(cc >= halo, has_prev), jnp.logical_or(cc < halo + CHUNK, has_next))
    shift = jnp.where(jnp.logical_and(cc == want, inside), 1.0, 0.0).astype(BF16)
    shifted = _dot(shift, src)
    acc = cbias_ref[...] + cw_ref[pad:pad + 1, :] * cur_ref[...].astype(F32)
    for n, k in enumerate(taps):
        acc = acc + cw_ref[k:k + 1, :] * shifted[n * CHUNK:(n + 1) * CHUNK]
    act = _silu(acc)
    act_b = act.astype(BF16)
    act_ref[...] = act_b

    dt = _softplus(dt_ref[...] + dtb_ref[...])
    a_row = -jnp.exp(alog_ref[...])
    _ssd_chunk(act[:, :D_INNER], act_b[:, D_INNER:D_INNER + GN], act_b[:, D_INNER + GN:], dt, a_row,
               e_ref, state_ref, yb_ref, 1)


def _ssd_fwd_kernel(act_ref, dt_ref, yb_ref, z_ref, dtb_ref, alog_ref, dskip_ref, nw_ref, e_ref,
                    o_ref, yf_ref, state_ref):
    s = pl.program_id(1)

    @pl.when(s == 0)
    def _():
        state_ref[...] = jnp.zeros_like(state_ref)

    act_b = act_ref[...]
    x = act_b[:, :D_INNER].astype(F32)
    dt = _softplus(dt_ref[...] + dtb_ref[...])
    a_row = -jnp.exp(alog_ref[...])
    _ssd_chunk(x, act_b[:, D_INNER:D_INNER + GN], act_b[:, D_INNER + GN:], dt, a_row, e_ref, state_ref, yf_ref, 0)
    y = yf_ref[...] + yb_ref[...] + x * dskip_ref[...]
    y = y * _silu(z_ref[...].astype(F32))
    y = y * lax.rsqrt(jnp.mean(y * y, axis=-1, keepdims=True) + EPS)
    o_ref[...] = (y * nw_ref[...]).astype(BF16)


def _ssd_mixer(z, xbc, dt_raw, conv_w, conv_b, dt_bias, a_log, d_skip, norm_w, geo):
    nt = z.shape[0]
    bsz, nch, ncc = geo["batch"], geo["nch"], geo["ncc"]
    halo = 16
    hb = CHUNK // halo
    n_halo_blocks = nt // halo

    def chunk_bwd(b, s):
        return b * nch + jnp.where(s < ncc, ncc - 1 - s, nch - 1 - (s - ncc))

    def chunk_fwd(b, s):
        return b * nch + s

    cw = jnp.zeros((8, CONV_DIM), F32).at[:SSM_CONV].set(conv_w)
    dtb = jnp.zeros((1, LANES), F32).at[0, :2 * SSM_HEADS].set(dt_bias.reshape(-1))
    alog = jnp.zeros((1, LANES), F32).at[0, :2 * SSM_HEADS].set(a_log.reshape(-1))
    heads = np.arange(D_INNER) // SSM_HEADDIM
    e_np = np.zeros((2, LANES, D_INNER), np.float32)
    for d in range(2):
        e_np[d, 32 * d + heads, np.arange(D_INNER)] = 1.0
    e_mats = jnp.asarray(e_np, BF16)
    const = lambda shape: pl.BlockSpec(shape, lambda b, s: (0,) * len(shape))

    act, yb = pl.pallas_call(
        functools.partial(_ssd_bwd_kernel, ncc=ncc, nch=nch),
        grid=(bsz, nch),
        in_specs=[
            pl.BlockSpec((CHUNK, CONV_DIM), lambda b, s: (chunk_bwd(b, s), 0)),
            pl.BlockSpec((halo, CONV_DIM), lambda b, s: (jnp.maximum(chunk_bwd(b, s) * hb - 1, 0), 0)),
            pl.BlockSpec((halo, CONV_DIM), lambda b, s: (jnp.minimum((chunk_bwd(b, s) + 1) * hb, n_halo_blocks - 1), 0)),
            pl.BlockSpec((CHUNK, LANES), lambda b, s: (chunk_bwd(b, s), 0)),
            const((8, CONV_DIM)),
            const((1, CONV_DIM)),
            const((1, LANES)),
            const((1, LANES)),
            pl.BlockSpec((None, LANES, D_INNER), lambda b, s: (1, 0, 0)),
        ],
        out_specs=[
            pl.BlockSpec((CHUNK, CONV_DIM), lambda b, s: (chunk_bwd(b, s), 0)),
            pl.BlockSpec((CHUNK, D_INNER), lambda b, s: (chunk_bwd(b, s), 0)),
        ],
        out_shape=[jax.ShapeDtypeStruct((nt, CONV_DIM), BF16), jax.ShapeDtypeStruct((nt, D_INNER), F32)],
        scratch_shapes=[pltpu.VMEM((SSM_GROUPS, SSM_STATE, D_INNER // SSM_GROUPS), F32)],
        compiler_params=_cp(2),
        name="ssd_backward_pass",
    )(xbc, xbc, xbc, dt_raw, cw, conv_b.reshape(1, CONV_DIM), dtb, alog, e_mats)

    return pl.pallas_call(
        _ssd_fwd_kernel,
        grid=(bsz, nch),
        in_specs=[
            pl.BlockSpec((CHUNK, CONV_DIM), lambda b, s: (chunk_fwd(b, s), 0)),
            pl.BlockSpec((CHUNK, LANES), lambda b, s: (chunk_fwd(b, s), 0)),
            pl.BlockSpec((CHUNK, D_INNER), lambda b, s: (chunk_fwd(b, s), 0)),
            pl.BlockSpec((CHUNK, D_INNER), lambda b, s: (chunk_fwd(b, s), 0)),
            const((1, LANES)),
            const((1, LANES)),
            const((1, D_INNER)),
            const((1, D_INNER)),
            pl.BlockSpec((None, LANES, D_INNER), lambda b, s: (0, 0, 0)),
        ],
        out_specs=pl.BlockSpec((CHUNK, D_INNER), lambda b, s: (chunk_fwd(b, s), 0)),
        out_shape=jax.ShapeDtypeStruct((nt, D_INNER), BF16),
        scratch_shapes=[
            pltpu.VMEM((CHUNK, D_INNER), F32),
            pltpu.VMEM((SSM_GROUPS, SSM_STATE, D_INNER // SSM_GROUPS), F32),
        ],
        compiler_params=_cp(2),
        name="ssd_forward_pass",
    )(act, dt_raw, yb, z, dtb, alog, jnp.repeat(d_skip, SSM_HEADDIM).reshape(1, D_INNER),
      norm_w.reshape(1, D_INNER), e_mats)


def _attn_kernel(sink_ref, q_ref, kc_ref, vc_ref, kp_ref, ko_ref, kn_ref, vp_ref, vo_ref, vn_ref, o_ref,
                 *, ncc, n_lat_blocks):
    blk = pl.program_id(1)
    n = blk - ncc
    is_lat = n >= 0
    q = CHUNK
    ii = lax.broadcasted_iota(jnp.int32, (q, q), 0)
    jj = lax.broadcasted_iota(jnp.int32, (q, q), 1)
    ok_prev = jnp.logical_and(jj >= ii, jnp.logical_and(is_lat, n >= 1))
    ok_own = jnp.logical_and(jj >= 0, is_lat)
    ok_next = jnp.logical_and(jj <= ii, jnp.logical_and(is_lat, n <= n_lat_blocks - 2))
    n_ctx = kc_ref.shape[0]
    lane = lax.broadcasted_iota(jnp.int32, (q, LANES), 1)
    low = lane < HEAD_DIM
    zero = jnp.zeros((q, LANES), BF16)
    for kh in range(N_KV_HEADS):
        ks = slice(kh * LANES, (kh + 1) * LANES)
        kcat = jnp.concatenate([kc_ref[:, ks], kp_ref[:, ks], ko_ref[:, ks], kn_ref[:, ks]], axis=0)
        vcat = jnp.concatenate([vc_ref[:, ks], vp_ref[:, ks], vo_ref[:, ks], vn_ref[:, ks]], axis=0)
        lhs = []
        for m in range(2):
            qp = q_ref[:, (2 * kh + m) * LANES:(2 * kh + m + 1) * LANES]
            lhs += [jnp.where(low, qp, zero), jnp.where(low, zero, qp)]
        s_all = _dot_nt(jnp.concatenate(lhs, axis=0), kcat)
        ps, inv = [], []
        for gi in range(4):
            sink = sink_ref[kh * 4 + gi]
            sh = s_all[gi * q:(gi + 1) * q]
            sc = jnp.concatenate([sh[:, :n_ctx],
                                  jnp.where(ok_prev, sh[:, n_ctx:n_ctx + q], NEG),
                                  jnp.where(ok_own, sh[:, n_ctx + q:n_ctx + 2 * q], NEG),
                                  jnp.where(ok_next, sh[:, n_ctx + 2 * q:], NEG)], axis=1)
            mx = jnp.maximum(jnp.max(sc, axis=-1, keepdims=True), sink)
            p = jnp.exp(sc - mx)
            denom = jnp.sum(p, axis=-1, keepdims=True) + jnp.exp(sink - mx)
            ps.append(p.astype(BF16))
            inv.append(1.0 / denom)
        r = _dot(jnp.concatenate(ps, axis=0), vcat)
        for m in range(2):
            o = jnp.where(low, r[(2 * m) * q:(2 * m + 1) * q] * inv[2 * m],
                          r[(2 * m + 1) * q:(2 * m + 2) * q] * inv[2 * m + 1])
            o_ref[:, (2 * kh + m) * LANES:(2 * kh + m + 1) * LANES] = o.astype(BF16)


def _attention(qr, kd, vd, sink, geo):
    nt = qr.shape[0]
    bsz, nch, ncc = geo["batch"], geo["nch"], geo["ncc"]
    nlb = nch - ncc
    kvw = kd.shape[1]
    ctx_rows = ncc * CHUNK

    def win(o):
        return lambda b, j, *_: (b * nch + ncc + jnp.clip(j - ncc + o - 1, 0, nlb - 1), 0)

    grid_spec = pltpu.PrefetchScalarGridSpec(
        num_scalar_prefetch=1,
        grid=(bsz, nch),
        in_specs=[
            pl.BlockSpec((CHUNK, qr.shape[1]), lambda b, j, *_: (b * nch + j, 0)),
            pl.BlockSpec((ctx_rows, kvw), lambda b, j, *_: (b * (nch // ncc), 0)),
            pl.BlockSpec((ctx_rows, kvw), lambda b, j, *_: (b * (nch // ncc), 0)),
            pl.BlockSpec((CHUNK, kvw), win(0)),
            pl.BlockSpec((CHUNK, kvw), win(1)),
            pl.BlockSpec((CHUNK, kvw), win(2)),
            pl.BlockSpec((CHUNK, kvw), win(0)),
            pl.BlockSpec((CHUNK, kvw), win(1)),
            pl.BlockSpec((CHUNK, kvw), win(2)),
        ],
        out_specs=pl.BlockSpec((CHUNK, qr.shape[1]), lambda b, j, *_: (b * nch + j, 0)),
    )
    return pl.pallas_call(
        functools.partial(_attn_kernel, ncc=ncc, n_lat_blocks=nlb),
        grid_spec=grid_spec,
        out_shape=jax.ShapeDtypeStruct((nt, qr.shape[1]), BF16),
        compiler_params=_cp(2, "parallel"),
        name="window_attention",
    )(sink, qr, kd, vd, kd, kd, kd, vd, vd, vd)


RUN_ALIGN = 8
SORTED_ROWS = 2 * TILE + N_EXPERTS * RUN_ALIGN
RUN_BITS = 6


def _run_copies(meta_s, make_copy, wait):
    for e in range(N_EXPERTS):
        units = meta_s[e, 0] // RUN_ALIGN
        local = meta_s[e, 1]
        glob = meta_s[e, 2]
        for b in range(RUN_BITS - 1, -1, -1):
            size = RUN_ALIGN << b
            done = ((units >> (b + 1)) << (b + 1)) * RUN_ALIGN

            @pl.when(((units >> b) & 1) == 1)
            def _():
                cp = make_copy(pl.multiple_of(local + done, RUN_ALIGN), pl.multiple_of(glob + done, RUN_ALIGN), size)
                if wait:
                    cp.wait()
                else:
                    cp.start()


def _route_kernel(h_ref, nw_ref, sh_ref, sc_ref, rwt_ref, rb_ref, wg_ref, wu_ref, wd_ref, xs_ref,
                  shared_ref, wcol_ref, meta_ref, cnt_ref,
                  sorted_ref, meta_v_ref, meta_s_ref, carry_ref, sem, *, cap):
    i = pl.program_id(0)
    t = TILE

    @pl.when(i == 0)
    def _():
        carry_ref[...] = jnp.zeros_like(carry_ref)

    m = _rms_mod(h_ref[...], nw_ref[...], sh_ref[...], sc_ref[...])
    logits = lax.dot_general(rwt_ref[...], m, (((1,), (1,)), ((), ())), preferred_element_type=F32,
                             precision=lax.Precision.HIGHEST)
    scores = _sigmoid(logits)
    biased = scores + rb_ref[...]
    rows = [biased[e:e + 1, :] for e in range(N_EXPERTS)]
    srows = [scores[e:e + 1, :] for e in range(N_EXPERTS)]
    gscore = []
    for g in range(N_EXPERT_GROUPS):
        r = rows[g * 4:(g + 1) * 4]
        best = None
        for a in range(4):
            for b in range(a + 1, 4):
                pair = r[a] + r[b]
                best = pair if best is None else jnp.maximum(best, pair)
        gscore.append(best)
    gbest = jnp.maximum(jnp.maximum(gscore[0], gscore[1]), jnp.maximum(gscore[2], gscore[3]))
    gsel = jnp.full((1, t), N_EXPERT_GROUPS - 1, jnp.int32)
    for g in range(N_EXPERT_GROUPS - 2, -1, -1):
        gsel = jnp.where(gscore[g] == gbest, g, gsel)
    cand = [jnp.where(gsel == e // 4, rows[e], NEG) for e in range(N_EXPERTS)]
    best1 = functools.reduce(jnp.maximum, cand)
    e1 = jnp.full((1, t), N_EXPERTS - 1, jnp.int32)
    for e in range(N_EXPERTS - 2, -1, -1):
        e1 = jnp.where(cand[e] == best1, e, e1)
    cand2 = [jnp.where(e1 == e, NEG, cand[e]) for e in range(N_EXPERTS)]
    best2 = functools.reduce(jnp.maximum, cand2)
    e2 = jnp.full((1, t), N_EXPERTS - 1, jnp.int32)
    for e in range(N_EXPERTS - 2, -1, -1):
        e2 = jnp.where(cand2[e] == best2, e, e2)
    s1 = functools.reduce(jnp.add, [jnp.where(e1 == e, srows[e], 0.0) for e in range(N_EXPERTS)])
    s2 = functools.reduce(jnp.add, [jnp.where(e2 == e, srows[e], 0.0) for e in range(N_EXPERTS)])
    wsum = s1 + s2
    w1 = s1 / wsum
    w2 = s2 / wsum
    eid = lax.broadcasted_iota(jnp.int32, (N_EXPERTS, t), 0)
    oh1 = eid == e1
    oh2 = eid == e2
    onehot = jnp.where(jnp.logical_or(oh1, oh2), 1.0, 0.0)
    jr = lax.broadcasted_iota(jnp.int32, (t, t), 0)
    jc = lax.broadcasted_iota(jnp.int32, (t, t), 1)
    before = jnp.where(jr < jc, 1.0, 0.0).astype(BF16)
    prefix = _dot(onehot.astype(BF16), before)
    n_e = jnp.sum(onehot, axis=1, keepdims=True)
    run = jnp.floor((n_e + (RUN_ALIGN - 1)) * (1.0 / RUN_ALIGN)) * RUN_ALIGN
    run_b = jnp.broadcast_to(run, (N_EXPERTS, LANES))
    er = lax.broadcasted_iota(jnp.int32, (N_EXPERTS, N_EXPERTS), 0)
    ec = lax.broadcasted_iota(jnp.int32, (N_EXPERTS, N_EXPERTS), 1)
    lower = jnp.where(ec < er, 1.0, 0.0).astype(BF16)
    local = _dot(lower, run_b.astype(BF16))
    eid_l = lax.broadcasted_iota(jnp.int32, (N_EXPERTS, LANES), 0)
    glob = (eid_l * cap).astype(F32) + carry_ref[...]
    carry_ref[...] = carry_ref[...] + run_b
    rl1 = jnp.sum(jnp.where(oh1, prefix + local[:, 0:1], 0.0), axis=0, keepdims=True)
    rl2 = jnp.sum(jnp.where(oh2, prefix + local[:, 0:1], 0.0), axis=0, keepdims=True)
    lane_l = lax.broadcasted_iota(jnp.int32, (N_EXPERTS, LANES), 1)
    meta = jnp.where(lane_l == 0, run_b, jnp.where(lane_l == 1, local, jnp.where(lane_l == 2, glob, 0.0)))
    meta_ref[...] = meta.astype(jnp.int32)
    meta_v_ref[...] = meta.astype(jnp.int32)
    cnt_ref[...] = carry_ref[...].astype(jnp.int32)
    wrows = jnp.concatenate([w1, w2, rl1, rl2, jnp.zeros((LANES - 4, t), F32)], axis=0)
    wcol_ref[...] = wrows.T

    cp = pltpu.make_async_copy(meta_v_ref, meta_s_ref, sem.at[1])
    cp.start()
    rr = lax.broadcasted_iota(jnp.int32, (SORTED_ROWS, t), 0)
    perm = jnp.where(jnp.logical_or(rr == rl1.astype(jnp.int32), rr == rl2.astype(jnp.int32)), 1.0, 0.0)
    mb = m.astype(BF16)
    sorted_ref[...] = _dot(perm.astype(BF16), mb)
    cp.wait()

    def make_copy(local_row, global_row, size):
        return pltpu.make_async_copy(sorted_ref.at[pl.ds(local_row, size)], xs_ref.at[pl.ds(global_row, size)],
                                     sem.at[0])

    _run_copies(meta_s_ref, make_copy, wait=False)
    hid = (_silu(_dot(mb, wg_ref[...])) * _dot(mb, wu_ref[...])).astype(BF16)
    shared_ref[...] = _dot(hid, wd_ref[...])
    _run_copies(meta_s_ref, make_copy, wait=True)


def _expert_kernel(blk_ref, exp_ref, nv_ref, new_ref, x_ref, wg_ref, wu_ref, wd_ref, y_ref, wgb_ref, wub_ref, wdb_ref):
    j = pl.program_id(0)
    nv = nv_ref[j]

    @pl.when(new_ref[j] == 1)
    def _():
        wgb_ref[...] = wg_ref[...].astype(BF16)
        wub_ref[...] = wu_ref[...].astype(BF16)
        wdb_ref[...] = wd_ref[...].astype(BF16)

    @pl.when(nv > 0)
    def _():
        rows = lax.broadcasted_iota(jnp.int32, (TILE, 1), 0)
        x = jnp.where(rows < nv, x_ref[...], 0.0).astype(BF16)
        hid = (_silu(_dot(x, wgb_ref[...])) * _dot(x, wub_ref[...])).astype(BF16)
        y_ref[...] = _dot(hid, wdb_ref[...])

    @pl.when(nv <= 0)
    def _():
        y_ref[...] = jnp.zeros_like(y_ref)


def _combine_kernel(h_ref, shared_ref, g_ref, wcol_ref, meta_ref, ys_ref, o_ref, buf_ref, meta_v_ref, meta_s_ref,
                    sem):
    t = TILE
    meta_v_ref[...] = meta_ref[...]
    cp = pltpu.make_async_copy(meta_v_ref, meta_s_ref, sem.at[1])
    cp.start()
    cp.wait()

    def make_copy(local_row, global_row, size):
        return pltpu.make_async_copy(ys_ref.at[pl.ds(global_row, size)], buf_ref.at[pl.ds(local_row, size)],
                                     sem.at[0])

    _run_copies(meta_s_ref, make_copy, wait=False)
    _run_copies(meta_s_ref, make_copy, wait=True)
    total = meta_s_ref[N_EXPERTS - 1, 0] + meta_s_ref[N_EXPERTS - 1, 1]
    rows = lax.broadcasted_iota(jnp.int32, (SORTED_ROWS, 1), 0)
    yb = jnp.where(rows < total, buf_ref[...], 0.0).astype(BF16)
    w = wcol_ref[...]
    cols = lax.broadcasted_iota(jnp.int32, (t, SORTED_ROWS), 1)
    pick1 = jnp.where(cols == w[:, 2:3].astype(jnp.int32), 1.0, 0.0).astype(BF16)
    pick2 = jnp.where(cols == w[:, 3:4].astype(jnp.int32), 1.0, 0.0).astype(BF16)
    routed = w[:, 0:1] * _dot(pick1, yb) + w[:, 1:2] * _dot(pick2, yb)
    o_ref[...] = h_ref[...] + g_ref[...] * (shared_ref[...] + routed)


def _moe(h, nw, mod, router_wt, router_b, layer, wg, wu, wd, swg, swu, swd, geo):
    nt, d = h.shape
    ntiles = nt // TILE
    row = geo["mod_row"]
    cap = -(-(nt + ntiles * (RUN_ALIGN - 1)) // TILE) * TILE
    dump_blk = N_EXPERTS * cap // TILE
    n_rows = N_EXPERTS * cap + TILE
    const = lambda shape: pl.BlockSpec(shape, lambda i: (0,) * len(shape))

    xs, shared, wcol, meta, cnt = pl.pallas_call(
        functools.partial(_route_kernel, cap=cap),
        grid=(ntiles,),
        in_specs=[
            pl.BlockSpec((TILE, d), lambda i: (i, 0)),
            const((1, d)),
            pl.BlockSpec((None, 1, d), lambda i: (row(i), 0, 3)),
            pl.BlockSpec((None, 1, d), lambda i: (row(i), 0, 4)),
            const((N_EXPERTS, d)),
            const((N_EXPERTS, 1)),
            const(swg.shape), const(swu.shape), const(swd.shape),
        ],
        out_specs=[
            pl.BlockSpec(memory_space=pl.ANY),
            pl.BlockSpec((TILE, d), lambda i: (i, 0)),
            pl.BlockSpec((TILE, LANES), lambda i: (i, 0)),
            pl.BlockSpec((None, N_EXPERTS, LANES), lambda i: (i, 0, 0)),
            const((N_EXPERTS, LANES)),
        ],
        out_shape=[
            jax.ShapeDtypeStruct((n_rows, d), F32),
            jax.ShapeDtypeStruct((nt, d), F32),
            jax.ShapeDtypeStruct((nt, LANES), F32),
            jax.ShapeDtypeStruct((ntiles, N_EXPERTS, LANES), jnp.int32),
            jax.ShapeDtypeStruct((N_EXPERTS, LANES), jnp.int32),
        ],
        scratch_shapes=[
            pltpu.VMEM((SORTED_ROWS, d), F32),
            pltpu.VMEM((N_EXPERTS, LANES), jnp.int32),
            pltpu.SMEM((N_EXPERTS, LANES), jnp.int32),
            pltpu.VMEM((N_EXPERTS, LANES), F32),
            pltpu.SemaphoreType.DMA((2,)),
        ],
        compiler_params=_cp(1),
        name="moe_route",
    )(h, nw.reshape(1, d), mod, mod, router_wt, router_b.reshape(N_EXPERTS, 1), swg, swu, swd)

    counts = cnt[:, 0]
    tiles_e = (counts + TILE - 1) // TILE
    ends = jnp.cumsum(tiles_e)
    starts = ends - tiles_e
    n_sched = -(-(2 * nt + ntiles * N_EXPERTS * (RUN_ALIGN - 1)) // TILE) + N_EXPERTS
    jidx = jnp.arange(n_sched, dtype=jnp.int32)
    e_of = jnp.minimum(jnp.sum((jidx[:, None] >= ends[None, :]).astype(jnp.int32), axis=1), N_EXPERTS - 1)
    local = jidx - starts[e_of]
    active = jidx < ends[-1]
    last_e = e_of[jnp.maximum(ends[-1] - 1, 0)]
    tile_blk = jnp.where(active, e_of * (cap // TILE) + local, dump_blk).astype(jnp.int32)
    tile_e = jnp.where(active, e_of, last_e).astype(jnp.int32)
    tile_nv = jnp.where(active, jnp.minimum(counts[e_of] - local * TILE, TILE), 0).astype(jnp.int32)
    tile_new = jnp.logical_and(active, local == 0).astype(jnp.int32)

    ys = pl.pallas_call(
        _expert_kernel,
        grid_spec=pltpu.PrefetchScalarGridSpec(
            num_scalar_prefetch=4,
            grid=(n_sched,),
            in_specs=[
                pl.BlockSpec((TILE, d), lambda j, blk, ex, nv, new: (blk[j], 0)),
                pl.BlockSpec((None, None, d, D_EXPERT), lambda j, blk, ex, nv, new: (layer, ex[j], 0, 0)),
                pl.BlockSpec((None, None, d, D_EXPERT), lambda j, blk, ex, nv, new: (layer, ex[j], 0, 0)),
                pl.BlockSpec((None, None, D_EXPERT, d), lambda j, blk, ex, nv, new: (layer, ex[j], 0, 0)),
            ],
            out_specs=pl.BlockSpec((TILE, d), lambda j, blk, ex, nv, new: (blk[j], 0)),
            scratch_shapes=[
                pltpu.VMEM((d, D_EXPERT), BF16),
                pltpu.VMEM((d, D_EXPERT), BF16),
                pltpu.VMEM((D_EXPERT, d), BF16),
            ],
        ),
        out_shape=jax.ShapeDtypeStruct((n_rows, d), F32),
        compiler_params=_cp(1),
        name="moe_experts",
    )(tile_blk, tile_e, tile_nv, tile_new, xs, wg, wu, wd)

    return pl.pallas_call(
        _combine_kernel,
        grid=(ntiles,),
        in_specs=[
            pl.BlockSpec((TILE, d), lambda i: (i, 0)),
            pl.BlockSpec((TILE, d), lambda i: (i, 0)),
            pl.BlockSpec((None, 1, d), lambda i: (row(i), 0, 5)),
            pl.BlockSpec((TILE, LANES), lambda i: (i, 0)),
            pl.BlockSpec((None, N_EXPERTS, LANES), lambda i: (i, 0, 0)),
            pl.BlockSpec(memory_space=pl.ANY),
        ],
        out_specs=pl.BlockSpec((TILE, d), lambda i: (i, 0)),
        out_shape=jax.ShapeDtypeStruct((nt, d), F32),
        scratch_shapes=[
            pltpu.VMEM((SORTED_ROWS, d), F32),
            pltpu.VMEM((N_EXPERTS, LANES), jnp.int32),
            pltpu.SMEM((N_EXPERTS, LANES), jnp.int32),
            pltpu.SemaphoreType.DMA((2,)),
        ],
        compiler_params=_cp(1),
        name="moe_combine",
    )(h, shared, mod, wcol, meta, ys)


def _final_kernel(h_ref, w_ref, o_ref):
    x = h_ref[...]
    o_ref[...] = x * lax.rsqrt(jnp.mean(x * x, axis=-1, keepdims=True) + EPS) * w_ref[...]


def _final_norm(h, w, geo):
    d = h.shape[1]
    tpb, bsz = geo["tpb"], geo["batch"]
    lat = tpb - 1
    return pl.pallas_call(
        _final_kernel,
        grid=(bsz * lat,),
        in_specs=[
            pl.BlockSpec((TILE, d), lambda i: ((i // lat) * tpb + 1 + i % lat, 0)),
            pl.BlockSpec((1, d), lambda i: (0, 0)),
        ],
        out_specs=pl.BlockSpec((TILE, d), lambda i: (i, 0)),
        out_shape=jax.ShapeDtypeStruct((bsz * lat * TILE, d), F32),
        compiler_params=_cp(1, "parallel"),
        name="final_norm",
    )(h, w.reshape(1, d))


def _rope_tables(n_ctx, seq):
    lane = np.arange(LANES) % HEAD_DIM
    axis = lane // (2 * ROPE_PAIRS)
    pair = lane % ROPE_PAIRS
    sign = np.where((lane % (2 * ROPE_PAIRS)) < ROPE_PAIRS, -1.0, 1.0).astype(np.float32)
    inv_freq = ROPE_BASE ** (-jnp.arange(ROPE_PAIRS, dtype=F32) / ROPE_PAIRS)
    t = jnp.arange(seq)
    posn = jnp.stack([t // GRID_W, t % GRID_W], axis=-1).astype(F32)
    ang = posn[:, axis] * inv_freq[pair][None, :]
    cos = jnp.concatenate([jnp.ones((n_ctx, LANES), F32), jnp.cos(ang)], axis=0)
    sin = jnp.concatenate([jnp.zeros((n_ctx, LANES), F32), jnp.sin(ang) * sign[None, :]], axis=0)
    return cos, sin


def kernel(x, c, ctx, c_ctx, ada_w, ada_b, norm1_w, norm2_w, ssd_in_w, ssd_conv_w, ssd_conv_b, ssd_dt_bias,
           ssd_a_log, ssd_d, ssd_norm_w, ssd_out_w, attn_qkv_w, attn_sink, attn_out_w, router_w, router_bias,
           moe_w_gate, moe_w_up, moe_w_down, shared_w_gate, shared_w_up, shared_w_down, final_norm_w):
    bsz, seq, d = x.shape
    n_ctx = ctx.shape[1]
    assert n_ctx == TILE and seq % TILE == 0 and d == D_MODEL and bsz < MOD_ROWS
    tpb = (n_ctx + seq) // TILE
    geo = {
        "batch": bsz,
        "tpb": tpb,
        "nch": (n_ctx + seq) // CHUNK,
        "ncc": n_ctx // CHUNK,
        "mod_row": lambda i: jnp.where(i % tpb == 0, bsz, i // tpb),
    }
    nt = bsz * (n_ctx + seq)
    h = jnp.concatenate([ctx, x], axis=1).reshape(nt, d)

    cc = jnp.zeros((MOD_ROWS, d), F32).at[:bsz].set(c).at[bsz].set(c_ctx)
    mod = _ada(cc, ada_w, ada_b).reshape(DEPTH, MOD_ROWS, 1, 6 * d)
    cos, sin = _rope_tables(n_ctx, seq)
    router_wt = router_w.T

    for l in range(DEPTH):
        j = l // 2
        if l % 2 == 0:
            w_in = ssd_in_w[j].astype(BF16)
            w_z = w_in[:, :D_INNER]
            w_xbc = w_in[:, D_INNER:D_INNER + CONV_DIM]
            w_dt = jnp.zeros((d, LANES), BF16).at[:, :2 * SSM_HEADS].set(w_in[:, D_INNER + CONV_DIM:])
            z, xbc, dt_raw = _norm_mod_matmul(h, norm1_w[l], mod[l], 0, [w_z, w_xbc, w_dt], [BF16, BF16, F32], geo)
            yn = _ssd_mixer(z, xbc, dt_raw, ssd_conv_w[j], ssd_conv_b[j], ssd_dt_bias[j], ssd_a_log[j],
                            ssd_d[j], ssd_norm_w[j], geo)
            h = _matmul_residual(yn, ssd_out_w[j].astype(BF16), h, mod[l], 2, geo)
        else:
            w_qkv = attn_qkv_w[j].astype(BF16)
            qd = N_Q_HEADS * HEAD_DIM
            kd = N_KV_HEADS * HEAD_DIM
            dup = lambda w: jnp.repeat(w.reshape(d, N_KV_HEADS, 1, HEAD_DIM), 2, axis=2).reshape(d, 2 * kd)
            w_q = w_qkv[:, :qd]
            w_k = dup(w_qkv[:, qd:qd + kd])
            w_v = dup(w_qkv[:, qd + kd:])
            qr, kdup, vdup = _norm_mod_matmul(h, norm1_w[l], mod[l], 0, [w_q, w_k, w_v], [BF16, BF16, BF16], geo,
                                              rope=(True, True, False), scales=(HEAD_DIM ** -0.5, 1.0, 1.0),
                                              tables=(cos, sin))
            o = _attention(qr, kdup, vdup, attn_sink[j], geo)
            h = _matmul_residual(o, attn_out_w[j].astype(BF16), h, mod[l], 2, geo)
        h = _moe(h, norm2_w[l], mod[l], router_wt, router_bias, l, moe_w_gate, moe_w_up, moe_w_down,
                 shared_w_gate[l].astype(BF16), shared_w_up[l].astype(BF16), shared_w_down[l].astype(BF16), geo)

    return _final_norm(h, final_norm_w, geo).reshape(bsz, seq, d)
```

```python
import functools

import jax
import jax.numpy as jnp
import numpy as np
from jax import lax
from jax.experimental import pallas as pl
from jax.experimental.pallas import tpu as pltpu

F32 = jnp.float32
BF16 = jnp.bfloat16

D_MODEL = 1024
DEPTH = 4
EPS = 1e-6
GRID_W = 64

D_INNER = 2048
SSM_HEADDIM = 64
SSM_HEADS = 32
SSM_STATE = 128
SSM_GROUPS = 4
SSM_CONV = 5
GN = SSM_GROUPS * SSM_STATE
CONV_DIM = D_INNER + 2 * GN
CHUNK = 128

HEAD_DIM = 64
N_Q_HEADS = 16
N_KV_HEADS = 4
ROPE_BASE = 10000.0
ROPE_PAIRS = 16
WINDOW = 128

N_EXPERTS = 16
N_EXPERT_GROUPS = 4
EXPERTS_PER_GROUP = 4
D_EXPERT = 512

TILE = 256
LANES = 128
MOD_ROWS = 16
NEG = -1e30
VMEM_LIMIT = 56 * 1024 * 1024


def _cp(n_axes, sem="arbitrary"):
    return pltpu.CompilerParams(dimension_semantics=(sem,) * n_axes, vmem_limit_bytes=VMEM_LIMIT)


def _dot(a, b):
    return jnp.dot(a, b, preferred_element_type=F32)


def _dot_nt(a, b):
    return lax.dot_general(a, b, (((1,), (1,)), ((), ())), preferred_element_type=F32)


def _split_bf16(v):
    hi = v.astype(BF16)
    lo = (v - hi.astype(F32)).astype(BF16)
    return hi, lo


def _sigmoid(x):
    return 1.0 / (1.0 + jnp.exp(-x))


def _silu(x):
    return x * _sigmoid(x)


def _softplus(x):
    return jnp.maximum(x, 0.0) + jnp.log(1.0 + jnp.exp(-jnp.abs(x)))


def _rms_mod(x, nw, shift, scale):
    y = x * lax.rsqrt(jnp.mean(x * x, axis=-1, keepdims=True) + EPS)
    return (y * nw) * (1.0 + scale) + shift


def _ada_kernel(c_ref, w_ref, b_ref, o_ref):
    s = _silu(c_ref[...])
    o_ref[...] = jnp.dot(s, w_ref[...], preferred_element_type=F32,
                         precision=lax.Precision.HIGHEST) + b_ref[...]


def _ada(cc, ada_w, ada_b):
    depth, d, n = ada_w.shape
    tn = 1024
    return pl.pallas_call(
        _ada_kernel,
        grid=(depth, n // tn),
        in_specs=[
            pl.BlockSpec((MOD_ROWS, d), lambda l, j: (0, 0)),
            pl.BlockSpec((None, d, tn), lambda l, j: (l, 0, j)),
            pl.BlockSpec((None, 1, tn), lambda l, j: (l, 0, j)),
        ],
        out_specs=pl.BlockSpec((None, MOD_ROWS, tn), lambda l, j: (l, 0, j)),
        out_shape=jax.ShapeDtypeStruct((depth, MOD_ROWS, n), F32),
        compiler_params=_cp(2),
        name="ada_table",
    )(cc, ada_w, ada_b.reshape(depth, 1, n))


def _rope128(x, cos, sin_signed, low_half):
    partner = jnp.where(low_half, pltpu.roll(x, LANES - ROPE_PAIRS, axis=1), pltpu.roll(x, ROPE_PAIRS, axis=1))
    return x * cos + partner * sin_signed


def _nmm_kernel(*refs, n_out, rope, scales):
    x_ref, nw_ref, sh_ref, sc_ref = refs[:4]
    w_refs = refs[4:4 + n_out]
    pos = 4 + n_out
    if any(rope):
        cos_ref, sin_ref = refs[pos:pos + 2]
        pos += 2
    o_refs = refs[pos:pos + n_out]
    a = _rms_mod(x_ref[...], nw_ref[...], sh_ref[...], sc_ref[...]).astype(BF16)
    if any(rope):
        lane = lax.broadcasted_iota(jnp.int32, (TILE, LANES), 1)
        low_half = (lane % (2 * ROPE_PAIRS)) < ROPE_PAIRS
        cos = cos_ref[...]
        sin = sin_ref[...]
    for k in range(n_out):
        if rope[k]:
            n = o_refs[k].shape[1]
            for j in range(n // LANES):
                acc = _dot(a, w_refs[k][:, j * LANES:(j + 1) * LANES])
                acc = _rope128(acc, cos, sin, low_half) * scales[k]
                o_refs[k][:, j * LANES:(j + 1) * LANES] = acc.astype(o_refs[k].dtype)
        else:
            o_refs[k][...] = (_dot(a, w_refs[k][...]) * scales[k]).astype(o_refs[k].dtype)


def _norm_mod_matmul(x, nw, mod, part, weights, out_dtypes, geo, rope=None, scales=None, tables=None):
    nt, d = x.shape
    n_out = len(weights)
    rope = tuple(rope) if rope is not None else (False,) * n_out
    scales = tuple(scales) if scales is not None else (1.0,) * n_out
    row = geo["mod_row"]
    in_specs = [
        pl.BlockSpec((TILE, d), lambda i: (i, 0)),
        pl.BlockSpec((1, d), lambda i: (0, 0)),
        pl.BlockSpec((None, 1, d), lambda i: (row(i), 0, part)),
        pl.BlockSpec((None, 1, d), lambda i: (row(i), 0, part + 1)),
    ]
    args = [x, nw.reshape(1, d), mod, mod]
    for w in weights:
        in_specs.append(pl.BlockSpec(w.shape, lambda i: (0, 0)))
        args.append(w)
    if any(rope):
        tpb = geo["tpb"]
        in_specs += [pl.BlockSpec((TILE, LANES), lambda i: (i % tpb, 0))] * 2
        args += list(tables)
    return pl.pallas_call(
        functools.partial(_nmm_kernel, n_out=n_out, rope=rope, scales=scales),
        grid=(nt // TILE,),
        in_specs=in_specs,
        out_specs=[pl.BlockSpec((TILE, w.shape[1]), lambda i: (i, 0)) for w in weights],
        out_shape=[jax.ShapeDtypeStruct((nt, w.shape[1]), dt) for w, dt in zip(weights, out_dtypes)],
        compiler_params=_cp(1, "parallel"),
        name="norm_proj_rope" if any(rope) else "norm_proj",
    )(*args)


def _mmres_kernel(a_ref, w_ref, h_ref, g_ref, o_ref):
    o_ref[...] = h_ref[...] + g_ref[...] * _dot(a_ref[...], w_ref[...])


def _matmul_residual(a, w, h, mod, part, geo, latent_only=False):
    nt, k = a.shape
    d = w.shape[1]
    if latent_only:
        tpb, lat = geo["tpb"], geo["tpb"] - 1
        n_tiles = geo["batch"] * lat
        src = lambda i: (i // lat) * tpb + 1 + i % lat
        row = lambda i: i // lat
    else:
        n_tiles = nt // TILE
        src = lambda i: i
        row = geo["mod_row"]
    return pl.pallas_call(
        _mmres_kernel,
        grid=(n_tiles,),
        in_specs=[
            pl.BlockSpec((TILE, k), lambda i: (src(i), 0)),
            pl.BlockSpec((k, d), lambda i: (0, 0)),
            pl.BlockSpec((TILE, d), lambda i: (src(i), 0)),
            pl.BlockSpec((None, 1, d), lambda i: (row(i), 0, part)),
        ],
        out_specs=pl.BlockSpec((TILE, d), lambda i: (i, 0)),
        out_shape=jax.ShapeDtypeStruct((n_tiles * TILE, d), F32),
        compiler_params=_cp(1, "parallel"),
        name="proj_residual",
    )(a, w, h, mod)


def _ssd_chunk(x, bm, cm, dt, a_row, e_ref, state_ref, y_ref, direction):
    q = CHUNK
    ii = lax.broadcasted_iota(jnp.int32, (q, q), 0)
    jj = lax.broadcasted_iota(jnp.int32, (q, q), 1)
    tri = (jj <= ii) if direction == 0 else (jj >= ii)
    tri_b = jnp.where(tri, 1.0, 0.0).astype(BF16)
    da = dt * a_row
    da_hi, da_lo = _split_bf16(da)
    cum = _dot(tri_b, da_hi) + _dot(tri_b, da_lo)
    cum_t = cum.T
    e = e_ref[...]

    def expand(v):
        hi, lo = _split_bf16(v)
        return _dot(hi, e) + _dot(lo, e)

    dt_x = expand(dt)
    cum_x = expand(cum)
    edge = q - 1 if direction == 0 else 0
    tot_x = cum_x[edge:edge + 1, :]
    dtx = x * dt_x
    dtx_b = dtx.astype(BF16)
    to_end = (dtx * jnp.exp(tot_x - cum_x)).astype(BF16)
    from_start = jnp.exp(cum_x)
    chunk_decay = jnp.exp(tot_x)
    lane = lax.broadcasted_iota(jnp.int32, (q, LANES), 1)
    first_head = lane < SSM_HEADDIM
    gw = D_INNER // SSM_GROUPS
    for g in range(SSM_GROUPS):
        bg = bm[:, g * SSM_STATE:(g + 1) * SSM_STATE]
        cg = cm[:, g * SSM_STATE:(g + 1) * SSM_STATE]
        cb = _dot_nt(cg, bg)
        h_prev = state_ref[g]
        y_off = _dot(cg, h_prev.astype(BF16)) * from_start[:, g * gw:(g + 1) * gw]
        bg_t = bg.astype(F32).T.astype(BF16)
        state_ref[g] = h_prev * chunk_decay[:, g * gw:(g + 1) * gw] + _dot(bg_t, to_end[:, g * gw:(g + 1) * gw])
        for p in range(gw // LANES):
            ms = []
            for k in range(2):
                hl = 32 * direction + g * 8 + 2 * p + k
                seg = cum[:, hl:hl + 1] - cum_t[hl:hl + 1, :]
                decay = jnp.exp(jnp.where(tri, seg, NEG))
                ms.append((cb * decay).astype(BF16))
            lo = g * gw + p * LANES
            out2 = _dot(jnp.concatenate(ms, axis=0), dtx_b[:, lo:lo + LANES])
            y_diag = jnp.where(first_head, out2[:q], out2[q:])
            y_ref[:, lo:lo + LANES] = y_diag + y_off[:, p * LANES:(p + 1) * LANES]


def _ssd_bwd_kernel(cur_ref, prev_ref, next_ref, dt_ref, cw_ref, cbias_ref, dtb_ref, alog_ref, e_ref,
                    act_ref, yb_ref, state_ref, *, ncc, nch):
    s = pl.program_id(1)
    c = jnp.where(s < ncc, ncc - 1 - s, nch - 1 - (s - ncc))

    @pl.when(s == 0)
    def _():
        state_ref[...] = jnp.zeros_like(state_ref)

    has_prev = jnp.logical_and(c != 0, c != ncc)
    has_next = jnp.logical_and(c != ncc - 1, c != nch - 1)
    halo = prev_ref.shape[0]
    pad = (SSM_CONV - 1) // 2
    taps = [k for k in range(SSM_CONV) if k != pad]
    n_src = CHUNK + 2 * halo
    src = jnp.concatenate([prev_ref[...], cur_ref[...], next_ref[...]], axis=0)
    rr = lax.broadcasted_iota(jnp.int32, (len(taps) * CHUNK, n_src), 0)
    cc = lax.broadcasted_iota(jnp.int32, (len(taps) * CHUNK, n_src), 1)
    want = rr + (halo - pad)
    for n, k in enumerate(taps):
        want = jnp.where(rr >= n * CHUNK, rr - n * CHUNK + (halo - pad + k), want)
    inside = jnp.logical_and(jnp.logical_or(cc >= halo, has_prev), jnp.logical_or(cc < halo + CHUNK, has_next))
    shift = jnp.where(jnp.logical_and(cc == want, inside), 1.0, 0.0).astype(BF16)
    shifted = _dot(shift, src)
    acc = cbias_ref[...] + cw_ref[pad:pad + 1, :] * cur_ref[...].astype(F32)
    for n, k in enumerate(taps):
        acc = acc + cw_ref[k:k + 1, :] * shifted[n * CHUNK:(n + 1) * CHUNK]
    act = _silu(acc)
    act_b = act.astype(BF16)
    act_ref[...] = act_b

    dt = _softplus(dt_ref[...] + dtb_ref[...])
    a_row = -jnp.exp(alog_ref[...])
    _ssd_chunk(act[:, :D_INNER], act_b[:, D_INNER:D_INNER + GN], act_b[:, D_INNER + GN:], dt, a_row,
               e_ref, state_ref, yb_ref, 1)


def _ssd_fwd_kernel(act_ref, dt_ref, yb_ref, z_ref, dtb_ref, alog_ref, dskip_ref, nw_ref, e_ref,
                    o_ref, yf_ref, state_ref):
    s = pl.program_id(1)

    @pl.when(s == 0)
    def _():
        state_ref[...] = jnp.zeros_like(state_ref)

    act_b = act_ref[...]
    x = act_b[:, :D_INNER].astype(F32)
    dt = _softplus(dt_ref[...] + dtb_ref[...])
    a_row = -jnp.exp(alog_ref[...])
    _ssd_chunk(x, act_b[:, D_INNER:D_INNER + GN], act_b[:, D_INNER + GN:], dt, a_row, e_ref, state_ref, yf_ref, 0)
    y = yf_ref[...] + yb_ref[...] + x * dskip_ref[...]
    y = y * _silu(z_ref[...].astype(F32))
    y = y * lax.rsqrt(jnp.mean(y * y, axis=-1, keepdims=True) + EPS)
    o_ref[...] = (y * nw_ref[...]).astype(BF16)


def _ssd_mixer(z, xbc, dt_raw, conv_w, conv_b, dt_bias, a_log, d_skip, norm_w, geo):
    nt = z.shape[0]
    bsz, nch, ncc = geo["batch"], geo["nch"], geo["ncc"]
    halo = 16
    hb = CHUNK // halo
    n_halo_blocks = nt // halo

    def chunk_bwd(b, s):
        return b * nch + jnp.where(s < ncc, ncc - 1 - s, nch - 1 - (s - ncc))

    def chunk_fwd(b, s):
        return b * nch + s

    cw = jnp.zeros((8, CONV_DIM), F32).at[:SSM_CONV].set(conv_w)
    dtb = jnp.zeros((1, LANES), F32).at[0, :2 * SSM_HEADS].set(dt_bias.reshape(-1))
    alog = jnp.zeros((1, LANES), F32).at[0, :2 * SSM_HEADS].set(a_log.reshape(-1))
    heads = np.arange(D_INNER) // SSM_HEADDIM
    e_np = np.zeros((2, LANES, D_INNER), np.float32)
    for d in range(2):
        e_np[d, 32 * d + heads, np.arange(D_INNER)] = 1.0
    e_mats = jnp.asarray(e_np, BF16)
    const = lambda shape: pl.BlockSpec(shape, lambda b, s: (0,) * len(shape))

    act, yb = pl.pallas_call(
        functools.partial(_ssd_bwd_kernel, ncc=ncc, nch=nch),
        grid=(bsz, nch),
        in_specs=[
            pl.BlockSpec((CHUNK, CONV_DIM), lambda b, s: (chunk_bwd(b, s), 0)),
            pl.BlockSpec((halo, CONV_DIM), lambda b, s: (jnp.maximum(chunk_bwd(b, s) * hb - 1, 0), 0)),
            pl.BlockSpec((halo, CONV_DIM), lambda b, s: (jnp.minimum((chunk_bwd(b, s) + 1) * hb, n_halo_blocks - 1), 0)),
            pl.BlockSpec((CHUNK, LANES), lambda b, s: (chunk_bwd(b, s), 0)),
            const((8, CONV_DIM)),
            const((1, CONV_DIM)),
            const((1, LANES)),
            const((1, LANES)),
            pl.BlockSpec((None, LANES, D_INNER), lambda b, s: (1, 0, 0)),
        ],
        out_specs=[
            pl.BlockSpec((CHUNK, CONV_DIM), lambda b, s: (chunk_bwd(b, s), 0)),
            pl.BlockSpec((CHUNK, D_INNER), lambda b, s: (chunk_bwd(b, s), 0)),
        ],
        out_shape=[jax.ShapeDtypeStruct((nt, CONV_DIM), BF16), jax.ShapeDtypeStruct((nt, D_INNER), F32)],
        scratch_shapes=[pltpu.VMEM((SSM_GROUPS, SSM_STATE, D_INNER // SSM_GROUPS), F32)],
        compiler_params=_cp(2),
        name="ssd_backward_pass",
    )(xbc, xbc, xbc, dt_raw, cw, conv_b.reshape(1, CONV_DIM), dtb, alog, e_mats)

    return pl.pallas_call(
        _ssd_fwd_kernel,
        grid=(bsz, nch),
        in_specs=[
            pl.BlockSpec((CHUNK, CONV_DIM), lambda b, s: (chunk_fwd(b, s), 0)),
            pl.BlockSpec((CHUNK, LANES), lambda b, s: (chunk_fwd(b, s), 0)),
            pl.BlockSpec((CHUNK, D_INNER), lambda b, s: (chunk_fwd(b, s), 0)),
            pl.BlockSpec((CHUNK, D_INNER), lambda b, s: (chunk_fwd(b, s), 0)),
            const((1, LANES)),
            const((1, LANES)),
            const((1, D_INNER)),
            const((1, D_INNER)),
            pl.BlockSpec((None, LANES, D_INNER), lambda b, s: (0, 0, 0)),
        ],
        out_specs=pl.BlockSpec((CHUNK, D_INNER), lambda b, s: (chunk_fwd(b, s), 0)),
        out_shape=jax.ShapeDtypeStruct((nt, D_INNER), BF16),
        scratch_shapes=[
            pltpu.VMEM((CHUNK, D_INNER), F32),
            pltpu.VMEM((SSM_GROUPS, SSM_STATE, D_INNER // SSM_GROUPS), F32),
        ],
        compiler_params=_cp(2),
        name="ssd_forward_pass",
    )(act, dt_raw, yb, z, dtb, alog, jnp.repeat(d_skip, SSM_HEADDIM).reshape(1, D_INNER),
      norm_w.reshape(1, D_INNER), e_mats)


def _attn_kernel(sink_ref, q_ref, kc_ref, vc_ref, kp_ref, ko_ref, kn_ref, vp_ref, vo_ref, vn_ref, o_ref,
                 *, ncc, n_lat_blocks, first):
    blk = pl.program_id(1) + first
    n = blk - ncc
    is_lat = n >= 0
    q = CHUNK
    ii = lax.broadcasted_iota(jnp.int32, (q, q), 0)
    jj = lax.broadcasted_iota(jnp.int32, (q, q), 1)
    ok_prev = jnp.logical_and(jj >= ii, jnp.logical_and(is_lat, n >= 1))
    ok_own = jnp.logical_and(jj >= 0, is_lat)
    ok_next = jnp.logical_and(jj <= ii, jnp.logical_and(is_lat, n <= n_lat_blocks - 2))
    n_ctx = kc_ref.shape[0]
    lane = lax.broadcasted_iota(jnp.int32, (q, LANES), 1)
    low = lane < HEAD_DIM
    zero = jnp.zeros((q, LANES), BF16)
    for kh in range(N_KV_HEADS):
        ks = slice(kh * LANES, (kh + 1) * LANES)
        kcat = jnp.concatenate([kc_ref[:, ks], kp_ref[:, ks], ko_ref[:, ks], kn_ref[:, ks]], axis=0)
        vcat = jnp.concatenate([vc_ref[:, ks], vp_ref[:, ks], vo_ref[:, ks], vn_ref[:, ks]], axis=0)
        lhs = []
        for m in range(2):
            qp = q_ref[:, (2 * kh + m) * LANES:(2 * kh + m + 1) * LANES]
            lhs += [jnp.where(low, qp, zero), jnp.where(low, zero, qp)]
        s_all = _dot_nt(jnp.concatenate(lhs, axis=0), kcat)
        ps, inv = [], []
        for gi in range(4):
            sink = sink_ref[kh * 4 + gi]
            sh = s_all[gi * q:(gi + 1) * q]
            sc = jnp.concatenate([sh[:, :n_ctx],
                                  jnp.where(ok_prev, sh[:, n_ctx:n_ctx + q], NEG),
                                  jnp.where(ok_own, sh[:, n_ctx + q:n_ctx + 2 * q], NEG),
                                  jnp.where(ok_next, sh[:, n_ctx + 2 * q:], NEG)], axis=1)
            mx = jnp.maximum(jnp.max(sc, axis=-1, keepdims=True), sink)
            p = jnp.exp(sc - mx)
            denom = jnp.sum(p, axis=-1, keepdims=True) + jnp.exp(sink - mx)
            ps.append(p.astype(BF16))
            inv.append(1.0 / denom)
        r = _dot(jnp.concatenate(ps, axis=0), vcat)
        for m in range(2):
            o = jnp.where(low, r[(2 * m) * q:(2 * m + 1) * q] * inv[2 * m],
                          r[(2 * m + 1) * q:(2 * m + 2) * q] * inv[2 * m + 1])
            o_ref[:, (2 * kh + m) * LANES:(2 * kh + m + 1) * LANES] = o.astype(BF16)


def _attention(qr, kd, vd, sink, geo, latent_only=False):
    nt = qr.shape[0]
    bsz, nch, ncc = geo["batch"], geo["nch"], geo["ncc"]
    nlb = nch - ncc
    kvw = kd.shape[1]
    ctx_rows = ncc * CHUNK
    first = ncc if latent_only else 0

    def win(o):
        return lambda b, j, *_: (b * nch + ncc + jnp.clip(j + first - ncc + o - 1, 0, nlb - 1), 0)

    grid_spec = pltpu.PrefetchScalarGridSpec(
        num_scalar_prefetch=1,
        grid=(bsz, nch - first),
        in_specs=[
            pl.BlockSpec((CHUNK, qr.shape[1]), lambda b, j, *_: (b * nch + first + j, 0)),
            pl.BlockSpec((ctx_rows, kvw), lambda b, j, *_: (b * (nch // ncc), 0)),
            pl.BlockSpec((ctx_rows, kvw), lambda b, j, *_: (b * (nch // ncc), 0)),
            pl.BlockSpec((CHUNK, kvw), win(0)),
            pl.BlockSpec((CHUNK, kvw), win(1)),
            pl.BlockSpec((CHUNK, kvw), win(2)),
            pl.BlockSpec((CHUNK, kvw), win(0)),
            pl.BlockSpec((CHUNK, kvw), win(1)),
            pl.BlockSpec((CHUNK, kvw), win(2)),
        ],
        out_specs=pl.BlockSpec((CHUNK, qr.shape[1]), lambda b, j, *_: (b * nch + first + j, 0)),
    )
    return pl.pallas_call(
        functools.partial(_attn_kernel, ncc=ncc, n_lat_blocks=nlb, first=first),
        grid_spec=grid_spec,
        out_shape=jax.ShapeDtypeStruct((nt, qr.shape[1]), BF16),
        compiler_params=_cp(2, "parallel"),
        name="window_attention",
    )(sink, qr, kd, vd, kd, kd, kd, vd, vd, vd)


RUN_ALIGN = 8
SORTED_ROWS = 2 * TILE + N_EXPERTS * RUN_ALIGN
RUN_BITS = 6


def _run_copies(meta, make_copy, wait):
    for e in range(N_EXPERTS):
        units = meta(e, 0) // RUN_ALIGN
        local = meta(e, 1)
        glob = meta(e, 2)
        for b in range(RUN_BITS - 1, -1, -1):
            size = RUN_ALIGN << b
            done = ((units >> (b + 1)) << (b + 1)) * RUN_ALIGN

            @pl.when(((units >> b) & 1) == 1)
            def _():
                cp = make_copy(pl.multiple_of(local + done, RUN_ALIGN), pl.multiple_of(glob + done, RUN_ALIGN), size)
                if wait:
                    cp.wait()
                else:
                    cp.start()


def _route_kernel(h_ref, nw_ref, sh_ref, sc_ref, rwt_ref, rb_ref, wg_ref, wu_ref, wd_ref, xs_ref,
                  shared_ref, wcol_ref, meta_ref, cnt_ref,
                  sorted_ref, meta_v_ref, meta_s_ref, carry_ref, sem, *, cap):
    i = pl.program_id(0)
    last = pl.num_programs(0) - 1
    slot = i % 2
    t = TILE

    @pl.when(i == 0)
    def _():
        carry_ref[...] = jnp.zeros_like(carry_ref)

    m = _rms_mod(h_ref[...], nw_ref[...], sh_ref[...], sc_ref[...])
    logits = lax.dot_general(rwt_ref[...], m, (((1,), (1,)), ((), ())), preferred_element_type=F32,
                             precision=lax.Precision.HIGHEST)
    scores = _sigmoid(logits)
    biased = scores + rb_ref[...]
    rows = [biased[e:e + 1, :] for e in range(N_EXPERTS)]
    srows = [scores[e:e + 1, :] for e in range(N_EXPERTS)]
    gscore = []
    for g in range(N_EXPERT_GROUPS):
        r = rows[g * 4:(g + 1) * 4]
        best = None
        for a in range(4):
            for b in range(a + 1, 4):
                pair = r[a] + r[b]
                best = pair if best is None else jnp.maximum(best, pair)
        gscore.append(best)
    gbest = jnp.maximum(jnp.maximum(gscore[0], gscore[1]), jnp.maximum(gscore[2], gscore[3]))
    gsel = jnp.full((1, t), N_EXPERT_GROUPS - 1, jnp.int32)
    for g in range(N_EXPERT_GROUPS - 2, -1, -1):
        gsel = jnp.where(gscore[g] == gbest, g, gsel)
    cand = [jnp.where(gsel == e // 4, rows[e], NEG) for e in range(N_EXPERTS)]
    best1 = functools.reduce(jnp.maximum, cand)
    e1 = jnp.full((1, t), N_EXPERTS - 1, jnp.int32)
    for e in range(N_EXPERTS - 2, -1, -1):
        e1 = jnp.where(cand[e] == best1, e, e1)
    cand2 = [jnp.where(e1 == e, NEG, cand[e]) for e in range(N_EXPERTS)]
    best2 = functools.reduce(jnp.maximum, cand2)
    e2 = jnp.full((1, t), N_EXPERTS - 1, jnp.int32)
    for e in range(N_EXPERTS - 2, -1, -1):
        e2 = jnp.where(cand2[e] == best2, e, e2)
    s1 = functools.reduce(jnp.add, [jnp.where(e1 == e, srows[e], 0.0) for e in range(N_EXPERTS)])
    s2 = functools.reduce(jnp.add, [jnp.where(e2 == e, srows[e], 0.0) for e in range(N_EXPERTS)])
    wsum = s1 + s2
    w1 = s1 / wsum
    w2 = s2 / wsum
    eid = lax.broadcasted_iota(jnp.int32, (N_EXPERTS, t), 0)
    oh1 = eid == e1
    oh2 = eid == e2
    onehot = jnp.where(jnp.logical_or(oh1, oh2), 1.0, 0.0)
    jr = lax.broadcasted_iota(jnp.int32, (t, t), 0)
    jc = lax.broadcasted_iota(jnp.int32, (t, t), 1)
    before = jnp.where(jr < jc, 1.0, 0.0).astype(BF16)
    prefix = _dot(onehot.astype(BF16), before)
    n_e = jnp.sum(onehot, axis=1, keepdims=True)
    run = jnp.floor((n_e + (RUN_ALIGN - 1)) * (1.0 / RUN_ALIGN)) * RUN_ALIGN
    run_b = jnp.broadcast_to(run, (N_EXPERTS, LANES))
    er = lax.broadcasted_iota(jnp.int32, (N_EXPERTS, N_EXPERTS), 0)
    ec = lax.broadcasted_iota(jnp.int32, (N_EXPERTS, N_EXPERTS), 1)
    lower = jnp.where(ec < er, 1.0, 0.0).astype(BF16)
    local = _dot(lower, run_b.astype(BF16))
    eid_l = lax.broadcasted_iota(jnp.int32, (N_EXPERTS, LANES), 0)
    glob = (eid_l * cap).astype(F32) + carry_ref[...]
    carry_ref[...] = carry_ref[...] + run_b
    rl1 = jnp.sum(jnp.where(oh1, prefix + local[:, 0:1], 0.0), axis=0, keepdims=True)
    rl2 = jnp.sum(jnp.where(oh2, prefix + local[:, 0:1], 0.0), axis=0, keepdims=True)
    lane_l = lax.broadcasted_iota(jnp.int32, (N_EXPERTS, LANES), 1)
    meta = jnp.where(lane_l == 0, run_b, jnp.where(lane_l == 1, local, jnp.where(lane_l == 2, glob, 0.0)))
    meta_ref[...] = meta.astype(jnp.int32)
    meta_v_ref[slot] = meta.astype(jnp.int32)
    cnt_ref[...] = carry_ref[...].astype(jnp.int32)
    wrows = jnp.concatenate([w1, w2, rl1, rl2, jnp.zeros((LANES - 4, t), F32)], axis=0)
    wcol_ref[...] = wrows.T

    meta_cp = pltpu.make_async_copy(meta_v_ref.at[slot], meta_s_ref.at[slot], sem.at[2 + slot])
    meta_cp.start()
    rr = lax.broadcasted_iota(jnp.int32, (SORTED_ROWS, t), 0)
    perm = jnp.where(jnp.logical_or(rr == rl1.astype(jnp.int32), rr == rl2.astype(jnp.int32)), 1.0, 0.0)
    mb = m.astype(BF16)
    sorted_ref[slot] = _dot(perm.astype(BF16), mb)

    def runs(s, wait):
        def make_copy(local_row, global_row, size):
            return pltpu.make_async_copy(sorted_ref.at[s, pl.ds(local_row, size)],
                                         xs_ref.at[pl.ds(global_row, size)], sem.at[s])
        _run_copies(lambda e, k: meta_s_ref[s, e, k], make_copy, wait)

    @pl.when(i > 0)
    def _():
        runs(1 - slot, True)

    meta_cp.wait()
    runs(slot, False)
    hid = (_silu(_dot(mb, wg_ref[...])) * _dot(mb, wu_ref[...])).astype(BF16)
    shared_ref[...] = _dot(hid, wd_ref[...])

    @pl.when(i == last)
    def _():
        runs(slot, True)


def _expert_kernel(blk_ref, exp_ref, nv_ref, new_ref, x_ref, wg_ref, wu_ref, wd_ref, y_ref, wgb_ref, wub_ref, wdb_ref):
    j = pl.program_id(0)
    nv = nv_ref[j]

    @pl.when(new_ref[j] == 1)
    def _():
        wgb_ref[...] = wg_ref[...].astype(BF16)
        wub_ref[...] = wu_ref[...].astype(BF16)
        wdb_ref[...] = wd_ref[...].astype(BF16)

    @pl.when(nv > 0)
    def _():
        rows = lax.broadcasted_iota(jnp.int32, (TILE, 1), 0)
        x = jnp.where(rows < nv, x_ref[...], 0.0).astype(BF16)
        hid = (_silu(_dot(x, wgb_ref[...])) * _dot(x, wub_ref[...])).astype(BF16)
        y_ref[...] = _dot(hid, wdb_ref[...])

    @pl.when(nv <= 0)
    def _():
        y_ref[...] = jnp.zeros_like(y_ref)


def _combine_kernel(h_ref, shared_ref, g_ref, wcol_ref, meta0_ref, meta1_ref, meta2_ref, ys_ref, o_ref,
                    buf_ref, meta_v_ref, meta_s_ref, sem):
    t = TILE
    i = pl.program_id(0)
    n = pl.num_programs(0)
    n_gather_sems = 2

    def table_copy(s):
        return pltpu.make_async_copy(meta_v_ref.at[s], meta_s_ref.at[s], sem.at[n_gather_sems + s])

    def gather(table, slot, wait):
        def make_copy(local_row, global_row, size):
            return pltpu.make_async_copy(ys_ref.at[pl.ds(global_row, size)],
                                         buf_ref.at[slot, pl.ds(local_row, size)], sem.at[slot])
        _run_copies(lambda e, k: meta_s_ref[table, e, k], make_copy, wait)

    @pl.when(i == 0)
    def _():
        meta_v_ref[0] = meta0_ref[...]
        table_copy(0).start()
        table_copy(0).wait()
        gather(0, 0, False)
        meta_v_ref[1] = meta1_ref[...]
        table_copy(1).start()

    @pl.when(i + 1 < n)
    def _():
        table_copy((i + 1) % 3).wait()
        gather((i + 1) % 3, (i + 1) % 2, False)

    @pl.when(i + 2 < n)
    def _():
        meta_v_ref[(i + 2) % 3] = meta2_ref[...]
        table_copy((i + 2) % 3).start()

    cur = i % 3
    gather(cur, i % 2, True)
    total = meta_s_ref[cur, N_EXPERTS - 1, 0] + meta_s_ref[cur, N_EXPERTS - 1, 1]
    rows = lax.broadcasted_iota(jnp.int32, (SORTED_ROWS, 1), 0)
    yb = jnp.where(rows < total, buf_ref[i % 2], 0.0).astype(BF16)
    w = wcol_ref[...]
    cols = lax.broadcasted_iota(jnp.int32, (t, SORTED_ROWS), 1)
    pick1 = jnp.where(cols == w[:, 2:3].astype(jnp.int32), 1.0, 0.0).astype(BF16)
    pick2 = jnp.where(cols == w[:, 3:4].astype(jnp.int32), 1.0, 0.0).astype(BF16)
    routed = w[:, 0:1] * _dot(pick1, yb) + w[:, 1:2] * _dot(pick2, yb)
    o_ref[...] = h_ref[...] + g_ref[...] * (shared_ref[...] + routed)


def _moe(h, nw, mod, router_wt, router_b, layer, wg, wu, wd, swg, swu, swd, geo):
    nt, d = h.shape
    ntiles = nt // TILE
    row = geo["mod_row"]
    cap = -(-(nt + ntiles * (RUN_ALIGN - 1)) // TILE) * TILE
    dump_blk = N_EXPERTS * cap // TILE
    n_rows = N_EXPERTS * cap + TILE
    const = lambda shape: pl.BlockSpec(shape, lambda i: (0,) * len(shape))

    xs, shared, wcol, meta, cnt = pl.pallas_call(
        functools.partial(_route_kernel, cap=cap),
        grid=(ntiles,),
        in_specs=[
            pl.BlockSpec((TILE, d), lambda i: (i, 0)),
            const((1, d)),
            pl.BlockSpec((None, 1, d), lambda i: (row(i), 0, 3)),
            pl.BlockSpec((None, 1, d), lambda i: (row(i), 0, 4)),
            const((N_EXPERTS, d)),
            const((N_EXPERTS, 1)),
            const(swg.shape), const(swu.shape), const(swd.shape),
        ],
        out_specs=[
            pl.BlockSpec(memory_space=pl.ANY),
            pl.BlockSpec((TILE, d), lambda i: (i, 0)),
            pl.BlockSpec((TILE, LANES), lambda i: (i, 0)),
            pl.BlockSpec((None, N_EXPERTS, LANES), lambda i: (i, 0, 0)),
            const((N_EXPERTS, LANES)),
        ],
        out_shape=[
            jax.ShapeDtypeStruct((n_rows, d), F32),
            jax.ShapeDtypeStruct((nt, d), F32),
            jax.ShapeDtypeStruct((nt, LANES), F32),
            jax.ShapeDtypeStruct((ntiles, N_EXPERTS, LANES), jnp.int32),
            jax.ShapeDtypeStruct((N_EXPERTS, LANES), jnp.int32),
        ],
        scratch_shapes=[
            pltpu.VMEM((2, SORTED_ROWS, d), F32),
            pltpu.VMEM((2, N_EXPERTS, LANES), jnp.int32),
            pltpu.SMEM((2, N_EXPERTS, LANES), jnp.int32),
            pltpu.VMEM((N_EXPERTS, LANES), F32),
            pltpu.SemaphoreType.DMA((4,)),
        ],
        compiler_params=_cp(1),
        name="moe_route",
    )(h, nw.reshape(1, d), mod, mod, router_wt, router_b.reshape(N_EXPERTS, 1), swg, swu, swd)

    counts = cnt[:, 0]
    tiles_e = (counts + TILE - 1) // TILE
    ends = jnp.cumsum(tiles_e)
    starts = ends - tiles_e
    n_sched = -(-(2 * nt + ntiles * N_EXPERTS * (RUN_ALIGN - 1)) // TILE) + N_EXPERTS
    jidx = jnp.arange(n_sched, dtype=jnp.int32)
    e_of = jnp.minimum(jnp.sum((jidx[:, None] >= ends[None, :]).astype(jnp.int32), axis=1), N_EXPERTS - 1)
    local = jidx - starts[e_of]
    active = jidx < ends[-1]
    last_e = e_of[jnp.maximum(ends[-1] - 1, 0)]
    tile_blk = jnp.where(active, e_of * (cap // TILE) + local, dump_blk).astype(jnp.int32)
    tile_e = jnp.where(active, e_of, last_e).astype(jnp.int32)
    tile_nv = jnp.where(active, jnp.minimum(counts[e_of] - local * TILE, TILE), 0).astype(jnp.int32)
    tile_new = jnp.logical_and(active, local == 0).astype(jnp.int32)

    ys = pl.pallas_call(
        _expert_kernel,
        grid_spec=pltpu.PrefetchScalarGridSpec(
            num_scalar_prefetch=4,
            grid=(n_sched,),
            in_specs=[
                pl.BlockSpec((TILE, d), lambda j, blk, ex, nv, new: (blk[j], 0)),
                pl.BlockSpec((None, None, d, D_EXPERT), lambda j, blk, ex, nv, new: (layer, ex[j], 0, 0)),
                pl.BlockSpec((None, None, d, D_EXPERT), lambda j, blk, ex, nv, new: (layer, ex[j], 0, 0)),
                pl.BlockSpec((None, None, D_EXPERT, d), lambda j, blk, ex, nv, new: (layer, ex[j], 0, 0)),
            ],
            out_specs=pl.BlockSpec((TILE, d), lambda j, blk, ex, nv, new: (blk[j], 0)),
            scratch_shapes=[
                pltpu.VMEM((d, D_EXPERT), BF16),
                pltpu.VMEM((d, D_EXPERT), BF16),
                pltpu.VMEM((D_EXPERT, d), BF16),
            ],
        ),
        out_shape=jax.ShapeDtypeStruct((n_rows, d), F32),
        compiler_params=_cp(1),
        name="moe_experts",
    )(tile_blk, tile_e, tile_nv, tile_new, xs, wg, wu, wd)

    return pl.pallas_call(
        _combine_kernel,
        grid=(ntiles,),
        in_specs=[
            pl.BlockSpec((TILE, d), lambda i: (i, 0)),
            pl.BlockSpec((TILE, d), lambda i: (i, 0)),
            pl.BlockSpec((None, 1, d), lambda i: (row(i), 0, 5)),
            pl.BlockSpec((TILE, LANES), lambda i: (i, 0)),
            pl.BlockSpec((None, N_EXPERTS, LANES), lambda i: (i, 0, 0)),
            pl.BlockSpec((None, N_EXPERTS, LANES), lambda i: (jnp.minimum(i + 1, ntiles - 1), 0, 0)),
            pl.BlockSpec((None, N_EXPERTS, LANES), lambda i: (jnp.minimum(i + 2, ntiles - 1), 0, 0)),
            pl.BlockSpec(memory_space=pl.ANY),
        ],
        out_specs=pl.BlockSpec((TILE, d), lambda i: (i, 0)),
        out_shape=jax.ShapeDtypeStruct((nt, d), F32),
        scratch_shapes=[
            pltpu.VMEM((2, SORTED_ROWS, d), F32),
            pltpu.VMEM((3, N_EXPERTS, LANES), jnp.int32),
            pltpu.SMEM((3, N_EXPERTS, LANES), jnp.int32),
            pltpu.SemaphoreType.DMA((5,)),
        ],
        compiler_params=_cp(1),
        name="moe_combine",
    )(h, shared, mod, wcol, meta, meta, meta, ys)


def _final_kernel(h_ref, w_ref, o_ref):
    x = h_ref[...]
    o_ref[...] = x * lax.rsqrt(jnp.mean(x * x, axis=-1, keepdims=True) + EPS) * w_ref[...]


def _final_norm(h, w):
    n, d = h.shape
    rows = 2 * TILE
    return pl.pallas_call(
        _final_kernel,
        grid=(n // rows,),
        in_specs=[
            pl.BlockSpec((rows, d), lambda i: (i, 0)),
            pl.BlockSpec((1, d), lambda i: (0, 0)),
        ],
        out_specs=pl.BlockSpec((rows, d), lambda i: (i, 0)),
        out_shape=jax.ShapeDtypeStruct((n, d), F32),
        compiler_params=_cp(1, "parallel"),
        name="final_norm",
    )(h, w.reshape(1, d))


def _rope_tables(n_ctx, seq):
    lane = np.arange(LANES) % HEAD_DIM
    axis = lane // (2 * ROPE_PAIRS)
    pair = lane % ROPE_PAIRS
    sign = np.where((lane % (2 * ROPE_PAIRS)) < ROPE_PAIRS, -1.0, 1.0).astype(np.float32)
    inv_freq = ROPE_BASE ** (-jnp.arange(ROPE_PAIRS, dtype=F32) / ROPE_PAIRS)
    t = jnp.arange(seq)
    posn = jnp.stack([t // GRID_W, t % GRID_W], axis=-1).astype(F32)
    ang = posn[:, axis] * inv_freq[pair][None, :]
    cos = jnp.concatenate([jnp.ones((n_ctx, LANES), F32), jnp.cos(ang)], axis=0)
    sin = jnp.concatenate([jnp.zeros((n_ctx, LANES), F32), jnp.sin(ang) * sign[None, :]], axis=0)
    return cos, sin


def kernel(x, c, ctx, c_ctx, ada_w, ada_b, norm1_w, norm2_w, ssd_in_w, ssd_conv_w, ssd_conv_b, ssd_dt_bias,
           ssd_a_log, ssd_d, ssd_norm_w, ssd_out_w, attn_qkv_w, attn_sink, attn_out_w, router_w, router_bias,
           moe_w_gate, moe_w_up, moe_w_down, shared_w_gate, shared_w_up, shared_w_down, final_norm_w):
    bsz, seq, d = x.shape
    n_ctx = ctx.shape[1]
    assert n_ctx == TILE and seq % TILE == 0 and d == D_MODEL and bsz < MOD_ROWS
    tpb = (n_ctx + seq) // TILE
    geo = {
        "batch": bsz,
        "tpb": tpb,
        "nch": (n_ctx + seq) // CHUNK,
        "ncc": n_ctx // CHUNK,
        "mod_row": lambda i: jnp.where(i % tpb == 0, bsz, i // tpb),
    }
    nt = bsz * (n_ctx + seq)
    h = jnp.concatenate([ctx, x], axis=1).reshape(nt, d)

    cc = jnp.zeros((MOD_ROWS, d), F32).at[:bsz].set(c).at[bsz].set(c_ctx)
    mod = _ada(cc, ada_w, ada_b).reshape(DEPTH, MOD_ROWS, 1, 6 * d)
    cos, sin = _rope_tables(n_ctx, seq)
    router_wt = router_w.T

    geo_lat = dict(geo, mod_row=lambda i: i // (tpb - 1))
    for l in range(DEPTH):
        j = l // 2
        last = l == DEPTH - 1
        if l % 2 == 0:
            w_in = ssd_in_w[j].astype(BF16)
            w_z = w_in[:, :D_INNER]
            w_xbc = w_in[:, D_INNER:D_INNER + CONV_DIM]
            w_dt = jnp.zeros((d, LANES), BF16).at[:, :2 * SSM_HEADS].set(w_in[:, D_INNER + CONV_DIM:])
            z, xbc, dt_raw = _norm_mod_matmul(h, norm1_w[l], mod[l], 0, [w_z, w_xbc, w_dt], [BF16, BF16, F32], geo)
            yn = _ssd_mixer(z, xbc, dt_raw, ssd_conv_w[j], ssd_conv_b[j], ssd_dt_bias[j], ssd_a_log[j],
                            ssd_d[j], ssd_norm_w[j], geo)
            h = _matmul_residual(yn, ssd_out_w[j].astype(BF16), h, mod[l], 2, geo, latent_only=last)
        else:
            w_qkv = attn_qkv_w[j].astype(BF16)
            qd = N_Q_HEADS * HEAD_DIM
            kd = N_KV_HEADS * HEAD_DIM
            dup = lambda w: jnp.repeat(w.reshape(d, N_KV_HEADS, 1, HEAD_DIM), 2, axis=2).reshape(d, 2 * kd)
            w_q = w_qkv[:, :qd]
            w_k = dup(w_qkv[:, qd:qd + kd])
            w_v = dup(w_qkv[:, qd + kd:])
            qr, kdup, vdup = _norm_mod_matmul(h, norm1_w[l], mod[l], 0, [w_q, w_k, w_v], [BF16, BF16, BF16], geo,
                                              rope=(True, True, False), scales=(HEAD_DIM ** -0.5, 1.0, 1.0),
                                              tables=(cos, sin))
            o = _attention(qr, kdup, vdup, attn_sink[j], geo, latent_only=last)
            h = _matmul_residual(o, attn_out_w[j].astype(BF16), h, mod[l], 2, geo, latent_only=last)
        h = _moe(h, norm2_w[l], mod[l], router_wt, router_bias, l, moe_w_gate, moe_w_up, moe_w_down,
                 shared_w_gate[l].astype(BF16), shared_w_up[l].astype(BF16), shared_w_down[l].astype(BF16),
                 geo_lat if last else geo)

    return _final_norm(h, final_norm_w).reshape(bsz, seq, d)
```

```python
import functools

import jax
import jax.numpy as jnp
import numpy as np
from jax import lax
from jax.experimental import pallas as pl
from jax.experimental.pallas import tpu as pltpu

F32 = jnp.float32
BF16 = jnp.bfloat16

D_MODEL = 1024
DEPTH = 4
EPS = 1e-6
GRID_W = 64

D_INNER = 2048
SSM_HEADDIM = 64
SSM_HEADS = 32
SSM_STATE = 128
SSM_GROUPS = 4
SSM_CONV = 5
GN = SSM_GROUPS * SSM_STATE
CONV_DIM = D_INNER + 2 * GN
CHUNK = 128

HEAD_DIM = 64
N_Q_HEADS = 16
N_KV_HEADS = 4
ROPE_BASE = 10000.0
ROPE_PAIRS = 16
WINDOW = 128

N_EXPERTS = 16
N_EXPERT_GROUPS = 4
EXPERTS_PER_GROUP = 4
D_EXPERT = 512

TILE = 256
LANES = 128
MOD_ROWS = 16
NEG = -1e30
VMEM_LIMIT = 56 * 1024 * 1024


def _cp(n_axes, sem="arbitrary"):
    return pltpu.CompilerParams(dimension_semantics=(sem,) * n_axes, vmem_limit_bytes=VMEM_LIMIT)


def _dot(a, b):
    return jnp.dot(a, b, preferred_element_type=F32)


def _dot_nt(a, b):
    return lax.dot_general(a, b, (((1,), (1,)), ((), ())), preferred_element_type=F32)


def _split_bf16(v):
    hi = v.astype(BF16)
    lo = (v - hi.astype(F32)).astype(BF16)
    return hi, lo


def _sigmoid(x):
    return 1.0 / (1.0 + jnp.exp(-x))


def _silu(x):
    return x * _sigmoid(x)


def _softplus(x):
    return jnp.maximum(x, 0.0) + jnp.log(1.0 + jnp.exp(-jnp.abs(x)))


def _rms_mod(x, nw, shift, scale):
    y = x * lax.rsqrt(jnp.mean(x * x, axis=-1, keepdims=True) + EPS)
    return (y * nw) * (1.0 + scale) + shift


def _ada_kernel(c_ref, w_ref, b_ref, o_ref):
    s = _silu(c_ref[...])
    o_ref[...] = jnp.dot(s, w_ref[...], preferred_element_type=F32,
                         precision=lax.Precision.HIGHEST) + b_ref[...]


def _ada(cc, ada_w, ada_b):
    depth, d, n = ada_w.shape
    tn = 1024
    return pl.pallas_call(
        _ada_kernel,
        grid=(depth, n // tn),
        in_specs=[
            pl.BlockSpec((MOD_ROWS, d), lambda l, j: (0, 0)),
            pl.BlockSpec((None, d, tn), lambda l, j: (l, 0, j)),
            pl.BlockSpec((None, 1, tn), lambda l, j: (l, 0, j)),
        ],
        out_specs=pl.BlockSpec((None, MOD_ROWS, tn), lambda l, j: (l, 0, j)),
        out_shape=jax.ShapeDtypeStruct((depth, MOD_ROWS, n), F32),
        compiler_params=_cp(2),
        name="ada_table",
    )(cc, ada_w, ada_b.reshape(depth, 1, n))


def _rope128(x, cos, sin_signed, low_half):
    partner = jnp.where(low_half, pltpu.roll(x, LANES - ROPE_PAIRS, axis=1), pltpu.roll(x, ROPE_PAIRS, axis=1))
    return x * cos + partner * sin_signed


def _nmm_kernel(*refs, n_out, rope, scales):
    x_ref, nw_ref, sh_ref, sc_ref = refs[:4]
    w_refs = refs[4:4 + n_out]
    pos = 4 + n_out
    if any(rope):
        cos_ref, sin_ref = refs[pos:pos + 2]
        pos += 2
    o_refs = refs[pos:pos + n_out]
    a = _rms_mod(x_ref[...], nw_ref[...], sh_ref[...], sc_ref[...]).astype(BF16)
    if any(rope):
        lane = lax.broadcasted_iota(jnp.int32, (TILE, LANES), 1)
        low_half = (lane % (2 * ROPE_PAIRS)) < ROPE_PAIRS
        cos = cos_ref[...]
        sin = sin_ref[...]
    for k in range(n_out):
        if rope[k]:
            n = o_refs[k].shape[1]
            for j in range(n // LANES):
                acc = _dot(a, w_refs[k][:, j * LANES:(j + 1) * LANES])
                acc = _rope128(acc, cos, sin, low_half) * scales[k]
                o_refs[k][:, j * LANES:(j + 1) * LANES] = acc.astype(o_refs[k].dtype)
        else:
            o_refs[k][...] = (_dot(a, w_refs[k][...]) * scales[k]).astype(o_refs[k].dtype)


def _norm_mod_matmul(x, nw, mod, part, weights, out_dtypes, geo, rope=None, scales=None, tables=None):
    nt, d = x.shape
    n_out = len(weights)
    rope = tuple(rope) if rope is not None else (False,) * n_out
    scales = tuple(scales) if scales is not None else (1.0,) * n_out
    row = geo["mod_row"]
    in_specs = [
        pl.BlockSpec((TILE, d), lambda i: (i, 0)),
        pl.BlockSpec((1, d), lambda i: (0, 0)),
        pl.BlockSpec((None, 1, d), lambda i: (row(i), 0, part)),
        pl.BlockSpec((None, 1, d), lambda i: (row(i), 0, part + 1)),
    ]
    args = [x, nw.reshape(1, d), mod, mod]
    for w in weights:
        in_specs.append(pl.BlockSpec(w.shape, lambda i: (0, 0)))
        args.append(w)
    if any(rope):
        tpb = geo["tpb"]
        in_specs += [pl.BlockSpec((TILE, LANES), lambda i: (i % tpb, 0))] * 2
        args += list(tables)
    return pl.pallas_call(
        functools.partial(_nmm_kernel, n_out=n_out, rope=rope, scales=scales),
        grid=(nt // TILE,),
        in_specs=in_specs,
        out_specs=[pl.BlockSpec((TILE, w.shape[1]), lambda i: (i, 0)) for w in weights],
        out_shape=[jax.ShapeDtypeStruct((nt, w.shape[1]), dt) for w, dt in zip(weights, out_dtypes)],
        compiler_params=_cp(1, "parallel"),
        name="norm_proj_rope" if any(rope) else "norm_proj",
    )(*args)


def _mmres_kernel(a_ref, w_ref, h_ref, g_ref, o_ref):
    o_ref[...] = h_ref[...] + g_ref[...] * _dot(a_ref[...], w_ref[...])


def _matmul_residual(a, w, h, mod, part, geo, latent_only=False):
    nt, k = a.shape
    d = w.shape[1]
    if latent_only:
        tpb, lat = geo["tpb"], geo["tpb"] - 1
        n_tiles = geo["batch"] * lat
        src = lambda i: (i // lat) * tpb + 1 + i % lat
        row = lambda i: i // lat
    else:
        n_tiles = nt // TILE
        src = lambda i: i
        row = geo["mod_row"]
    return pl.pallas_call(
        _mmres_kernel,
        grid=(n_tiles,),
        in_specs=[
            pl.BlockSpec((TILE, k), lambda i: (src(i), 0)),
            pl.BlockSpec((k, d), lambda i: (0, 0)),
            pl.BlockSpec((TILE, d), lambda i: (src(i), 0)),
            pl.BlockSpec((None, 1, d), lambda i: (row(i), 0, part)),
        ],
        out_specs=pl.BlockSpec((TILE, d), lambda i: (i, 0)),
        out_shape=jax.ShapeDtypeStruct((n_tiles * TILE, d), F32),
        compiler_params=_cp(1, "parallel"),
        name="proj_residual",
    )(a, w, h, mod)


def _ssd_chunk(x, bm, cm, dt, a_row, e_ref, state_ref, y_ref, direction):
    q = CHUNK
    ii = lax.broadcasted_iota(jnp.int32, (q, q), 0)
    jj = lax.broadcasted_iota(jnp.int32, (q, q), 1)
    tri = (jj <= ii) if direction == 0 else (jj >= ii)
    tri_b = jnp.where(tri, 1.0, 0.0).astype(BF16)
    da = dt * a_row
    da_hi, da_lo = _split_bf16(da)
    cum = _dot(tri_b, da_hi) + _dot(tri_b, da_lo)
    cum_t = cum.T
    e = e_ref[...]

    def expand(v):
        hi, lo = _split_bf16(v)
        return _dot(hi, e) + _dot(lo, e)

    dt_x = expand(dt)
    cum_x = expand(cum)
    edge = q - 1 if direction == 0 else 0
    tot_x = cum_x[edge:edge + 1, :]
    dtx = x * dt_x
    dtx_b = dtx.astype(BF16)
    to_end = (dtx * jnp.exp(tot_x - cum_x)).astype(BF16)
    from_start = jnp.exp(cum_x)
    chunk_decay = jnp.exp(tot_x)
    lane = lax.broadcasted_iota(jnp.int32, (q, LANES), 1)
    first_head = lane < SSM_HEADDIM
    gw = D_INNER // SSM_GROUPS
    for g in range(SSM_GROUPS):
        bg = bm[:, g * SSM_STATE:(g + 1) * SSM_STATE]
        cg = cm[:, g * SSM_STATE:(g + 1) * SSM_STATE]
        cb = _dot_nt(cg, bg)
        h_prev = state_ref[g]
        y_off = _dot(cg, h_prev.astype(BF16)) * from_start[:, g * gw:(g + 1) * gw]
        bg_t = bg.astype(F32).T.astype(BF16)
        state_ref[g] = h_prev * chunk_decay[:, g * gw:(g + 1) * gw] + _dot(bg_t, to_end[:, g * gw:(g + 1) * gw])
        for p in range(gw // LANES):
            ms = []
            for k in range(2):
                hl = 32 * direction + g * 8 + 2 * p + k
                seg = cum[:, hl:hl + 1] - cum_t[hl:hl + 1, :]
                decay = jnp.exp(jnp.where(tri, seg, NEG))
                ms.append((cb * decay).astype(BF16))
            lo = g * gw + p * LANES
            out2 = _dot(jnp.concatenate(ms, axis=0), dtx_b[:, lo:lo + LANES])
            y_diag = jnp.where(first_head, out2[:q], out2[q:])
            y_ref[:, lo:lo + LANES] = (y_diag + y_off[:, p * LANES:(p + 1) * LANES]).astype(y_ref.dtype)


def _ssd_bwd_kernel(cur_ref, prev_ref, next_ref, dt_ref, cw_ref, cbias_ref, dtb_ref, alog_ref, e_ref,
                    act_ref, yb_ref, state_ref, *, ncc, nch):
    s = pl.program_id(1)
    c = jnp.where(s < ncc, ncc - 1 - s, nch - 1 - (s - ncc))

    @pl.when(s == 0)
    def _():
        state_ref[...] = jnp.zeros_like(state_ref)

    has_prev = jnp.logical_and(c != 0, c != ncc)
    has_next = jnp.logical_and(c != ncc - 1, c != nch - 1)
    halo = prev_ref.shape[0]
    pad = (SSM_CONV - 1) // 2
    taps = [k for k in range(SSM_CONV) if k != pad]
    n_src = CHUNK + 2 * halo
    src = jnp.concatenate([prev_ref[...], cur_ref[...], next_ref[...]], axis=0)
    rr = lax.broadcasted_iota(jnp.int32, (len(taps) * CHUNK, n_src), 0)
    cc = lax.broadcasted_iota(jnp.int32, (len(taps) * CHUNK, n_src), 1)
    want = rr + (halo - pad)
    for n, k in enumerate(taps):
        want = jnp.where(rr >= n * CHUNK, rr - n * CHUNK + (halo - pad + k), want)
    inside = jnp.logical_and(jnp.logical_or(cc >= halo, has_prev), jnp.logical_or(cc < halo + CHUNK, has_next))
    shift = jnp.where(jnp.logical_and(cc == want, inside), 1.0, 0.0).astype(BF16)
    shifted = _dot(shift, src)
    acc = cbias_ref[...] + cw_ref[pad:pad + 1, :] * cur_ref[...].astype(F32)
    for n, k in enumerate(taps):
        acc = acc + cw_ref[k:k + 1, :] * shifted[n * CHUNK:(n + 1) * CHUNK]
    act = _silu(acc)
    act_b = act.astype(BF16)
    act_ref[...] = act_b

    dt = _softplus(dt_ref[...] + dtb_ref[...])
    a_row = -jnp.exp(alog_ref[...])
    _ssd_chunk(act[:, :D_INNER], act_b[:, D_INNER:D_INNER + GN], act_b[:, D_INNER + GN:], dt, a_row,
               e_ref, state_ref, yb_ref, 1)


def _ssd_fwd_kernel(act_ref, dt_ref, yb_ref, z_ref, dtb_ref, alog_ref, dskip_ref, nw_ref, e_ref,
                    o_ref, yf_ref, state_ref):
    s = pl.program_id(1)

    @pl.when(s == 0)
    def _():
        state_ref[...] = jnp.zeros_like(state_ref)

    act_b = act_ref[...]
    x = act_b[:, :D_INNER].astype(F32)
    dt = _softplus(dt_ref[...] + dtb_ref[...])
    a_row = -jnp.exp(alog_ref[...])
    _ssd_chunk(x, act_b[:, D_INNER:D_INNER + GN], act_b[:, D_INNER + GN:], dt, a_row, e_ref, state_ref, yf_ref, 0)
    y = yf_ref[...] + yb_ref[...] + x * dskip_ref[...]
    y = y * _silu(z_ref[...].astype(F32))
    y = y * lax.rsqrt(jnp.mean(y * y, axis=-1, keepdims=True) + EPS)
    o_ref[...] = (y * nw_ref[...]).astype(BF16)


def _ssd_mixer(z, xbc, dt_raw, conv_w, conv_b, dt_bias, a_log, d_skip, norm_w, geo):
    nt = z.shape[0]
    bsz, nch, ncc = geo["batch"], geo["nch"], geo["ncc"]
    halo = 16
    hb = CHUNK // halo
    n_halo_blocks = nt // halo

    def chunk_bwd(b, s):
        return b * nch + jnp.where(s < ncc, ncc - 1 - s, nch - 1 - (s - ncc))

    def chunk_fwd(b, s):
        return b * nch + s

    cw = jnp.zeros((8, CONV_DIM), F32).at[:SSM_CONV].set(conv_w)
    dtb = jnp.zeros((1, LANES), F32).at[0, :2 * SSM_HEADS].set(dt_bias.reshape(-1))
    alog = jnp.zeros((1, LANES), F32).at[0, :2 * SSM_HEADS].set(a_log.reshape(-1))
    heads = np.arange(D_INNER) // SSM_HEADDIM
    e_np = np.zeros((2, LANES, D_INNER), np.float32)
    for d in range(2):
        e_np[d, 32 * d + heads, np.arange(D_INNER)] = 1.0
    e_mats = jnp.asarray(e_np, BF16)
    const = lambda shape: pl.BlockSpec(shape, lambda b, s: (0,) * len(shape))

    act, yb = pl.pallas_call(
        functools.partial(_ssd_bwd_kernel, ncc=ncc, nch=nch),
        grid=(bsz, nch),
        in_specs=[
            pl.BlockSpec((CHUNK, CONV_DIM), lambda b, s: (chunk_bwd(b, s), 0)),
            pl.BlockSpec((halo, CONV_DIM), lambda b, s: (jnp.maximum(chunk_bwd(b, s) * hb - 1, 0), 0)),
            pl.BlockSpec((halo, CONV_DIM), lambda b, s: (jnp.minimum((chunk_bwd(b, s) + 1) * hb, n_halo_blocks - 1), 0)),
            pl.BlockSpec((CHUNK, LANES), lambda b, s: (chunk_bwd(b, s), 0)),
            const((8, CONV_DIM)),
            const((1, CONV_DIM)),
            const((1, LANES)),
            const((1, LANES)),
            pl.BlockSpec((None, LANES, D_INNER), lambda b, s: (1, 0, 0)),
        ],
        out_specs=[
            pl.BlockSpec((CHUNK, CONV_DIM), lambda b, s: (chunk_bwd(b, s), 0)),
            pl.BlockSpec((CHUNK, D_INNER), lambda b, s: (chunk_bwd(b, s), 0)),
        ],
        out_shape=[jax.ShapeDtypeStruct((nt, CONV_DIM), BF16), jax.ShapeDtypeStruct((nt, D_INNER), BF16)],
        scratch_shapes=[pltpu.VMEM((SSM_GROUPS, SSM_STATE, D_INNER // SSM_GROUPS), F32)],
        compiler_params=_cp(2),
        name="ssd_backward_pass",
    )(xbc, xbc, xbc, dt_raw, cw, conv_b.reshape(1, CONV_DIM), dtb, alog, e_mats)

    return pl.pallas_call(
        _ssd_fwd_kernel,
        grid=(bsz, nch),
        in_specs=[
            pl.BlockSpec((CHUNK, CONV_DIM), lambda b, s: (chunk_fwd(b, s), 0)),
            pl.BlockSpec((CHUNK, LANES), lambda b, s: (chunk_fwd(b, s), 0)),
            pl.BlockSpec((CHUNK, D_INNER), lambda b, s: (chunk_fwd(b, s), 0)),
            pl.BlockSpec((CHUNK, D_INNER), lambda b, s: (chunk_fwd(b, s), 0)),
            const((1, LANES)),
            const((1, LANES)),
            const((1, D_INNER)),
            const((1, D_INNER)),
            pl.BlockSpec((None, LANES, D_INNER), lambda b, s: (0, 0, 0)),
        ],
        out_specs=pl.BlockSpec((CHUNK, D_INNER), lambda b, s: (chunk_fwd(b, s), 0)),
        out_shape=jax.ShapeDtypeStruct((nt, D_INNER), BF16),
        scratch_shapes=[
            pltpu.VMEM((CHUNK, D_INNER), F32),
            pltpu.VMEM((SSM_GROUPS, SSM_STATE, D_INNER // SSM_GROUPS), F32),
        ],
        compiler_params=_cp(2),
        name="ssd_forward_pass",
    )(act, dt_raw, yb, z, dtb, alog, jnp.repeat(d_skip, SSM_HEADDIM).reshape(1, D_INNER),
      norm_w.reshape(1, D_INNER), e_mats)


def _attn_kernel(sink_ref, q_ref, kc_ref, vc_ref, kp_ref, ko_ref, kn_ref, vp_ref, vo_ref, vn_ref, o_ref,
                 *, ncc, n_lat_blocks, first):
    blk = pl.program_id(1) + first
    n = blk - ncc
    is_lat = n >= 0
    q = CHUNK
    ii = lax.broadcasted_iota(jnp.int32, (q, q), 0)
    jj = lax.broadcasted_iota(jnp.int32, (q, q), 1)
    ok_prev = jnp.logical_and(jj >= ii, jnp.logical_and(is_lat, n >= 1))
    ok_own = jnp.logical_and(jj >= 0, is_lat)
    ok_next = jnp.logical_and(jj <= ii, jnp.logical_and(is_lat, n <= n_lat_blocks - 2))
    n_ctx = kc_ref.shape[0]
    lane = lax.broadcasted_iota(jnp.int32, (q, LANES), 1)
    low = lane < HEAD_DIM
    zero = jnp.zeros((q, LANES), BF16)
    for kh in range(N_KV_HEADS):
        ks = slice(kh * LANES, (kh + 1) * LANES)
        kcat = jnp.concatenate([kc_ref[:, ks], kp_ref[:, ks], ko_ref[:, ks], kn_ref[:, ks]], axis=0)
        vcat = jnp.concatenate([vc_ref[:, ks], vp_ref[:, ks], vo_ref[:, ks], vn_ref[:, ks]], axis=0)
        lhs = []
        for m in range(2):
            qp = q_ref[:, (2 * kh + m) * LANES:(2 * kh + m + 1) * LANES]
            lhs += [jnp.where(low, qp, zero), jnp.where(low, zero, qp)]
        s_all = _dot_nt(jnp.concatenate(lhs, axis=0), kcat)
        ps, inv = [], []
        for gi in range(4):
            sink = sink_ref[kh * 4 + gi]
            sh = s_all[gi * q:(gi + 1) * q]
            sc = jnp.concatenate([sh[:, :n_ctx],
                                  jnp.where(ok_prev, sh[:, n_ctx:n_ctx + q], NEG),
                                  jnp.where(ok_own, sh[:, n_ctx + q:n_ctx + 2 * q], NEG),
                                  jnp.where(ok_next, sh[:, n_ctx + 2 * q:], NEG)], axis=1)
            mx = jnp.maximum(jnp.max(sc, axis=-1, keepdims=True), sink)
            p = jnp.exp(sc - mx)
            denom = jnp.sum(p, axis=-1, keepdims=True) + jnp.exp(sink - mx)
            ps.append(p.astype(BF16))
            inv.append(1.0 / denom)
        r = _dot(jnp.concatenate(ps, axis=0), vcat)
        for m in range(2):
            o = jnp.where(low, r[(2 * m) * q:(2 * m + 1) * q] * inv[2 * m],
                          r[(2 * m + 1) * q:(2 * m + 2) * q] * inv[2 * m + 1])
            o_ref[:, (2 * kh + m) * LANES:(2 * kh + m + 1) * LANES] = o.astype(BF16)


def _attention(qr, kd, vd, sink, geo, latent_only=False):
    nt = qr.shape[0]
    bsz, nch, ncc = geo["batch"], geo["nch"], geo["ncc"]
    nlb = nch - ncc
    kvw = kd.shape[1]
    ctx_rows = ncc * CHUNK
    first = ncc if latent_only else 0

    def win(o):
        return lambda b, j, *_: (b * nch + ncc + jnp.clip(j + first - ncc + o - 1, 0, nlb - 1), 0)

    grid_spec = pltpu.PrefetchScalarGridSpec(
        num_scalar_prefetch=1,
        grid=(bsz, nch - first),
        in_specs=[
            pl.BlockSpec((CHUNK, qr.shape[1]), lambda b, j, *_: (b * nch + first + j, 0)),
            pl.BlockSpec((ctx_rows, kvw), lambda b, j, *_: (b * (nch // ncc), 0)),
            pl.BlockSpec((ctx_rows, kvw), lambda b, j, *_: (b * (nch // ncc), 0)),
            pl.BlockSpec((CHUNK, kvw), win(0)),
            pl.BlockSpec((CHUNK, kvw), win(1)),
            pl.BlockSpec((CHUNK, kvw), win(2)),
            pl.BlockSpec((CHUNK, kvw), win(0)),
            pl.BlockSpec((CHUNK, kvw), win(1)),
            pl.BlockSpec((CHUNK, kvw), win(2)),
        ],
        out_specs=pl.BlockSpec((CHUNK, qr.shape[1]), lambda b, j, *_: (b * nch + first + j, 0)),
    )
    return pl.pallas_call(
        functools.partial(_attn_kernel, ncc=ncc, n_lat_blocks=nlb, first=first),
        grid_spec=grid_spec,
        out_shape=jax.ShapeDtypeStruct((nt, qr.shape[1]), BF16),
        compiler_params=_cp(2, "parallel"),
        name="window_attention",
    )(sink, qr, kd, vd, kd, kd, kd, vd, vd, vd)


RUN_ALIGN = 8
SORTED_ROWS = 2 * TILE + N_EXPERTS * RUN_ALIGN
RUN_BITS = 6
RUN_HEAD = 64


def _pack_bf16_pair(lo, hi):
    ulo = lax.bitcast_convert_type(lo, jnp.uint32)
    uhi = lax.bitcast_convert_type(hi, jnp.uint32)
    return (uhi & jnp.uint32(0xFFFF0000)) | (ulo >> 16)


def _unpack_bf16_pair(p):
    lo = lax.bitcast_convert_type(p << 16, F32)
    hi = lax.bitcast_convert_type(p & jnp.uint32(0xFFFF0000), F32)
    return lo.astype(BF16), hi.astype(BF16)


def _run_copies(meta, make_copy, wait, head=0):
    def act(cp):
        if wait:
            cp.wait()
        else:
            cp.start()

    for e in range(N_EXPERTS):
        rows = meta(e, 0)
        local = meta(e, 1)
        glob = meta(e, 2)

        def pieces(units, skip):
            for b in range(RUN_BITS - 1, -1, -1):
                size = RUN_ALIGN << b
                done = skip + ((units >> (b + 1)) << (b + 1)) * RUN_ALIGN

                @pl.when(((units >> b) & 1) == 1)
                def _():
                    act(make_copy(pl.multiple_of(local + done, RUN_ALIGN), pl.multiple_of(glob + done, RUN_ALIGN),
                                  size))

        if head:
            act(make_copy(pl.multiple_of(local, RUN_ALIGN), pl.multiple_of(glob, RUN_ALIGN), head))

            @pl.when(rows > head)
            def _():
                pieces((rows - head) // RUN_ALIGN, head)
        else:
            pieces(rows // RUN_ALIGN, 0)


def _route_kernel(h_ref, nw_ref, sh_ref, sc_ref, rwt_ref, rb_ref, wg_ref, wu_ref, wd_ref, xs_ref,
                  shared_ref, wcol_ref, meta_ref, cnt_ref,
                  sorted_ref, meta_v_ref, meta_s_ref, carry_ref, sem, *, cap):
    i = pl.program_id(0)
    last = pl.num_programs(0) - 1
    slot = i % 2
    t = TILE

    @pl.when(i == 0)
    def _():
        carry_ref[...] = jnp.zeros_like(carry_ref)
        sorted_ref[:, SORTED_ROWS:, :] = jnp.zeros((2, RUN_HEAD, sorted_ref.shape[2]), jnp.uint32)

    m = _rms_mod(h_ref[...], nw_ref[...], sh_ref[...], sc_ref[...])
    logits = lax.dot_general(rwt_ref[...], m, (((1,), (1,)), ((), ())), preferred_element_type=F32,
                             precision=lax.Precision.HIGHEST)
    scores = _sigmoid(logits)
    biased = scores + rb_ref[...]
    rows = [biased[e:e + 1, :] for e in range(N_EXPERTS)]
    srows = [scores[e:e + 1, :] for e in range(N_EXPERTS)]
    gscore = []
    for g in range(N_EXPERT_GROUPS):
        r = rows[g * 4:(g + 1) * 4]
        best = None
        for a in range(4):
            for b in range(a + 1, 4):
                pair = r[a] + r[b]
                best = pair if best is None else jnp.maximum(best, pair)
        gscore.append(best)
    gbest = jnp.maximum(jnp.maximum(gscore[0], gscore[1]), jnp.maximum(gscore[2], gscore[3]))
    gsel = jnp.full((1, t), N_EXPERT_GROUPS - 1, jnp.int32)
    for g in range(N_EXPERT_GROUPS - 2, -1, -1):
        gsel = jnp.where(gscore[g] == gbest, g, gsel)
    cand = [jnp.where(gsel == e // 4, rows[e], NEG) for e in range(N_EXPERTS)]
    best1 = functools.reduce(jnp.maximum, cand)
    e1 = jnp.full((1, t), N_EXPERTS - 1, jnp.int32)
    for e in range(N_EXPERTS - 2, -1, -1):
        e1 = jnp.where(cand[e] == best1, e, e1)
    cand2 = [jnp.where(e1 == e, NEG, cand[e]) for e in range(N_EXPERTS)]
    best2 = functools.reduce(jnp.maximum, cand2)
    e2 = jnp.full((1, t), N_EXPERTS - 1, jnp.int32)
    for e in range(N_EXPERTS - 2, -1, -1):
        e2 = jnp.where(cand2[e] == best2, e, e2)
    s1 = functools.reduce(jnp.add, [jnp.where(e1 == e, srows[e], 0.0) for e in range(N_EXPERTS)])
    s2 = functools.reduce(jnp.add, [jnp.where(e2 == e, srows[e], 0.0) for e in range(N_EXPERTS)])
    wsum = s1 + s2
    w1 = s1 / wsum
    w2 = s2 / wsum
    eid = lax.broadcasted_iota(jnp.int32, (N_EXPERTS, t), 0)
    oh1 = eid == e1
    oh2 = eid == e2
    onehot = jnp.where(jnp.logical_or(oh1, oh2), 1.0, 0.0)
    jr = lax.broadcasted_iota(jnp.int32, (t, t), 0)
    jc = lax.broadcasted_iota(jnp.int32, (t, t), 1)
    before = jnp.where(jr < jc, 1.0, 0.0).astype(BF16)
    prefix = _dot(onehot.astype(BF16), before)
    n_e = jnp.sum(onehot, axis=1, keepdims=True)
    run = jnp.floor((n_e + (RUN_ALIGN - 1)) * (1.0 / RUN_ALIGN)) * RUN_ALIGN
    run_b = jnp.broadcast_to(run, (N_EXPERTS, LANES))
    er = lax.broadcasted_iota(jnp.int32, (N_EXPERTS, N_EXPERTS), 0)
    ec = lax.broadcasted_iota(jnp.int32, (N_EXPERTS, N_EXPERTS), 1)
    lower = jnp.where(ec < er, 1.0, 0.0).astype(BF16)
    local = _dot(lower, run_b.astype(BF16))
    eid_l = lax.broadcasted_iota(jnp.int32, (N_EXPERTS, LANES), 0)
    glob = (eid_l * cap).astype(F32) + carry_ref[...]
    carry_ref[...] = carry_ref[...] + run_b
    rl1 = jnp.sum(jnp.where(oh1, prefix + local[:, 0:1], 0.0), axis=0, keepdims=True)
    rl2 = jnp.sum(jnp.where(oh2, prefix + local[:, 0:1], 0.0), axis=0, keepdims=True)
    lane_l = lax.broadcasted_iota(jnp.int32, (N_EXPERTS, LANES), 1)
    meta = jnp.where(lane_l == 0, run_b, jnp.where(lane_l == 1, local, jnp.where(lane_l == 2, glob, 0.0)))
    meta_ref[...] = meta.astype(jnp.int32)
    meta_v_ref[slot] = meta.astype(jnp.int32)
    cnt_ref[...] = carry_ref[...].astype(jnp.int32)
    wrows = jnp.concatenate([w1, w2, rl1, rl2, jnp.zeros((LANES - 4, t), F32)], axis=0)
    wcol_ref[...] = wrows.T

    meta_cp = pltpu.make_async_copy(meta_v_ref.at[slot], meta_s_ref.at[slot], sem.at[2 + slot])
    meta_cp.start()
    rr = lax.broadcasted_iota(jnp.int32, (SORTED_ROWS, t), 0)
    perm = jnp.where(jnp.logical_or(rr == rl1.astype(jnp.int32), rr == rl2.astype(jnp.int32)), 1.0, 0.0)
    mb = m.astype(BF16)
    srt = _dot(perm.astype(BF16), mb)
    half = sorted_ref.shape[2]
    sorted_ref[slot, 0:SORTED_ROWS, :] = _pack_bf16_pair(srt[:, :half], srt[:, half:])

    def runs(s, wait):
        def make_copy(local_row, global_row, size):
            return pltpu.make_async_copy(sorted_ref.at[s, pl.ds(local_row, size)],
                                         xs_ref.at[pl.ds(global_row, size)], sem.at[s])
        _run_copies(lambda e, k: meta_s_ref[s, e, k], make_copy, wait, head=RUN_HEAD)

    @pl.when(i > 0)
    def _():
        runs(1 - slot, True)

    meta_cp.wait()
    runs(slot, False)
    hid = (_silu(_dot(mb, wg_ref[...])) * _dot(mb, wu_ref[...])).astype(BF16)
    shared_ref[...] = _dot(hid, wd_ref[...])

    @pl.when(i == last)
    def _():
        runs(slot, True)


def _expert_kernel(blk_ref, exp_ref, nv_ref, new_ref, x_ref, wg_ref, wu_ref, wd_ref, y_ref, wgb_ref, wub_ref, wdb_ref):
    j = pl.program_id(0)
    nv = nv_ref[j]

    @pl.when(new_ref[j] == 1)
    def _():
        wgb_ref[...] = wg_ref[...].astype(BF16)
        wub_ref[...] = wu_ref[...].astype(BF16)
        wdb_ref[...] = wd_ref[...].astype(BF16)

    @pl.when(nv > 0)
    def _():
        rows = lax.broadcasted_iota(jnp.int32, (TILE, 1), 0)
        x_lo, x_hi = _unpack_bf16_pair(jnp.where(rows < nv, x_ref[...], jnp.uint32(0)))
        half = x_ref.shape[1]
        gate = _dot(x_lo, wgb_ref[0:half, :]) + _dot(x_hi, wgb_ref[half:, :])
        up = _dot(x_lo, wub_ref[0:half, :]) + _dot(x_hi, wub_ref[half:, :])
        hid = (_silu(gate) * up).astype(BF16)
        y = _dot(hid, wdb_ref[...]).astype(BF16).astype(F32)
        y_ref[...] = _pack_bf16_pair(y[:, :half], y[:, half:])

    @pl.when(nv <= 0)
    def _():
        y_ref[...] = jnp.zeros_like(y_ref)


def _combine_kernel(h_ref, shared_ref, g_ref, wcol_ref, meta0_ref, meta1_ref, meta2_ref, ys_ref, o_ref,
                    buf_ref, meta_v_ref, meta_s_ref, sem):
    t = TILE
    i = pl.program_id(0)
    n = pl.num_programs(0)
    n_gather_sems = 2

    def table_copy(s):
        return pltpu.make_async_copy(meta_v_ref.at[s], meta_s_ref.at[s], sem.at[n_gather_sems + s])

    def gather(table, slot, wait):
        def make_copy(local_row, global_row, size):
            return pltpu.make_async_copy(ys_ref.at[pl.ds(global_row, size)],
                                         buf_ref.at[slot, pl.ds(local_row, size)], sem.at[slot])
        _run_copies(lambda e, k: meta_s_ref[table, e, k], make_copy, wait)

    @pl.when(i == 0)
    def _():
        meta_v_ref[0] = meta0_ref[...]
        table_copy(0).start()
        table_copy(0).wait()
        gather(0, 0, False)
        meta_v_ref[1] = meta1_ref[...]
        table_copy(1).start()

    @pl.when(i + 1 < n)
    def _():
        table_copy((i + 1) % 3).wait()
        gather((i + 1) % 3, (i + 1) % 2, False)

    @pl.when(i + 2 < n)
    def _():
        meta_v_ref[(i + 2) % 3] = meta2_ref[...]
        table_copy((i + 2) % 3).start()

    cur = i % 3
    gather(cur, i % 2, True)
    total = meta_s_ref[cur, N_EXPERTS - 1, 0] + meta_s_ref[cur, N_EXPERTS - 1, 1]
    rows = lax.broadcasted_iota(jnp.int32, (SORTED_ROWS, 1), 0)
    y_lo, y_hi = _unpack_bf16_pair(jnp.where(rows < total, buf_ref[i % 2], jnp.uint32(0)))
    w = wcol_ref[...]
    cols = lax.broadcasted_iota(jnp.int32, (t, SORTED_ROWS), 1)
    pick1 = jnp.where(cols == w[:, 2:3].astype(jnp.int32), 1.0, 0.0).astype(BF16)
    pick2 = jnp.where(cols == w[:, 3:4].astype(jnp.int32), 1.0, 0.0).astype(BF16)
    routed = jnp.concatenate([w[:, 0:1] * _dot(pick1, y) + w[:, 1:2] * _dot(pick2, y) for y in (y_lo, y_hi)], axis=1)
    o_ref[...] = h_ref[...] + g_ref[...] * (shared_ref[...] + routed)


def _moe(h, nw, mod, router_wt, router_b, layer, wg, wu, wd, swg, swu, swd, geo):
    nt, d = h.shape
    ntiles = nt // TILE
    row = geo["mod_row"]
    cap = -(-(nt + ntiles * (RUN_ALIGN - 1) + RUN_HEAD) // TILE) * TILE
    dump_blk = N_EXPERTS * cap // TILE
    n_rows = N_EXPERTS * cap + TILE
    const = lambda shape: pl.BlockSpec(shape, lambda i: (0,) * len(shape))

    xs, shared, wcol, meta, cnt = pl.pallas_call(
        functools.partial(_route_kernel, cap=cap),
        grid=(ntiles,),
        in_specs=[
            pl.BlockSpec((TILE, d), lambda i: (i, 0)),
            const((1, d)),
            pl.BlockSpec((None, 1, d), lambda i: (row(i), 0, 3)),
            pl.BlockSpec((None, 1, d), lambda i: (row(i), 0, 4)),
            const((N_EXPERTS, d)),
            const((N_EXPERTS, 1)),
            const(swg.shape), const(swu.shape), const(swd.shape),
        ],
        out_specs=[
            pl.BlockSpec(memory_space=pl.ANY),
            pl.BlockSpec((TILE, d), lambda i: (i, 0)),
            pl.BlockSpec((TILE, LANES), lambda i: (i, 0)),
            pl.BlockSpec((None, N_EXPERTS, LANES), lambda i: (i, 0, 0)),
            const((N_EXPERTS, LANES)),
        ],
        out_shape=[
            jax.ShapeDtypeStruct((n_rows, d // 2), jnp.uint32),
            jax.ShapeDtypeStruct((nt, d), F32),
            jax.ShapeDtypeStruct((nt, LANES), F32),
            jax.ShapeDtypeStruct((ntiles, N_EXPERTS, LANES), jnp.int32),
            jax.ShapeDtypeStruct((N_EXPERTS, LANES), jnp.int32),
        ],
        scratch_shapes=[
            pltpu.VMEM((2, SORTED_ROWS + RUN_HEAD, d // 2), jnp.uint32),
            pltpu.VMEM((2, N_EXPERTS, LANES), jnp.int32),
            pltpu.SMEM((2, N_EXPERTS, LANES), jnp.int32),
            pltpu.VMEM((N_EXPERTS, LANES), F32),
            pltpu.SemaphoreType.DMA((4,)),
        ],
        compiler_params=_cp(1),
        name="moe_route",
    )(h, nw.reshape(1, d), mod, mod, router_wt, router_b.reshape(N_EXPERTS, 1), swg, swu, swd)

    counts = cnt[:, 0]
    tiles_e = (counts + TILE - 1) // TILE
    ends = jnp.cumsum(tiles_e)
    starts = ends - tiles_e
    n_sched = -(-(2 * nt + ntiles * N_EXPERTS * (RUN_ALIGN - 1)) // TILE) + N_EXPERTS
    jidx = jnp.arange(n_sched, dtype=jnp.int32)
    e_of = jnp.minimum(jnp.sum((jidx[:, None] >= ends[None, :]).astype(jnp.int32), axis=1), N_EXPERTS - 1)
    local = jidx - starts[e_of]
    active = jidx < ends[-1]
    last_e = e_of[jnp.maximum(ends[-1] - 1, 0)]
    tile_blk = jnp.where(active, e_of * (cap // TILE) + local, dump_blk).astype(jnp.int32)
    tile_e = jnp.where(active, e_of, last_e).astype(jnp.int32)
    tile_nv = jnp.where(active, jnp.minimum(counts[e_of] - local * TILE, TILE), 0).astype(jnp.int32)
    tile_new = jnp.logical_and(active, local == 0).astype(jnp.int32)

    ys = pl.pallas_call(
        _expert_kernel,
        grid_spec=pltpu.PrefetchScalarGridSpec(
            num_scalar_prefetch=4,
            grid=(n_sched,),
            in_specs=[
                pl.BlockSpec((TILE, d // 2), lambda j, blk, ex, nv, new: (blk[j], 0)),
                pl.BlockSpec((None, None, d, D_EXPERT), lambda j, blk, ex, nv, new: (layer, ex[j], 0, 0)),
                pl.BlockSpec((None, None, d, D_EXPERT), lambda j, blk, ex, nv, new: (layer, ex[j], 0, 0)),
                pl.BlockSpec((None, None, D_EXPERT, d), lambda j, blk, ex, nv, new: (layer, ex[j], 0, 0)),
            ],
            out_specs=pl.BlockSpec((TILE, d // 2), lambda j, blk, ex, nv, new: (blk[j], 0)),
            scratch_shapes=[
                pltpu.VMEM((d, D_EXPERT), BF16),
                pltpu.VMEM((d, D_EXPERT), BF16),
                pltpu.VMEM((D_EXPERT, d), BF16),
            ],
        ),
        out_shape=jax.ShapeDtypeStruct((n_rows, d // 2), jnp.uint32),
        compiler_params=_cp(1),
        name="moe_experts",
    )(tile_blk, tile_e, tile_nv, tile_new, xs, wg, wu, wd)

    return pl.pallas_call(
        _combine_kernel,
        grid=(ntiles,),
        in_specs=[
            pl.BlockSpec((TILE, d), lambda i: (i, 0)),
            pl.BlockSpec((TILE, d), lambda i: (i, 0)),
            pl.BlockSpec((None, 1, d), lambda i: (row(i), 0, 5)),
            pl.BlockSpec((TILE, LANES), lambda i: (i, 0)),
            pl.BlockSpec((None, N_EXPERTS, LANES), lambda i: (i, 0, 0)),
            pl.BlockSpec((None, N_EXPERTS, LANES), lambda i: (jnp.minimum(i + 1, ntiles - 1), 0, 0)),
            pl.BlockSpec((None, N_EXPERTS, LANES), lambda i: (jnp.minimum(i + 2, ntiles - 1), 0, 0)),
            pl.BlockSpec(memory_space=pl.ANY),
        ],
        out_specs=pl.BlockSpec((TILE, d), lambda i: (i, 0)),
        out_shape=jax.ShapeDtypeStruct((nt, d), F32),
        scratch_shapes=[
            pltpu.VMEM((2, SORTED_ROWS, d // 2), jnp.uint32),
            pltpu.VMEM((3, N_EXPERTS, LANES), jnp.int32),
            pltpu.SMEM((3, N_EXPERTS, LANES), jnp.int32),
            pltpu.SemaphoreType.DMA((5,)),
        ],
        compiler_params=_cp(1),
        name="moe_combine",
    )(h, shared, mod, wcol, meta, meta, meta, ys)


def _final_kernel(h_ref, w_ref, o_ref):
    x = h_ref[...]
    o_ref[...] = x * lax.rsqrt(jnp.mean(x * x, axis=-1, keepdims=True) + EPS) * w_ref[...]


def _final_norm(h, w):
    n, d = h.shape
    rows = 2 * TILE
    return pl.pallas_call(
        _final_kernel,
        grid=(n // rows,),
        in_specs=[
            pl.BlockSpec((rows, d), lambda i: (i, 0)),
            pl.BlockSpec((1, d), lambda i: (0, 0)),
        ],
        out_specs=pl.BlockSpec((rows, d), lambda i: (i, 0)),
        out_shape=jax.ShapeDtypeStruct((n, d), F32),
        compiler_params=_cp(1, "parallel"),
        name="final_norm",
    )(h, w.reshape(1, d))


def _rope_tables(n_ctx, seq):
    lane = np.arange(LANES) % HEAD_DIM
    axis = lane // (2 * ROPE_PAIRS)
    pair = lane % ROPE_PAIRS
    sign = np.where((lane % (2 * ROPE_PAIRS)) < ROPE_PAIRS, -1.0, 1.0).astype(np.float32)
    inv_freq = ROPE_BASE ** (-jnp.arange(ROPE_PAIRS, dtype=F32) / ROPE_PAIRS)
    t = jnp.arange(seq)
    posn = jnp.stack([t // GRID_W, t % GRID_W], axis=-1).astype(F32)
    ang = posn[:, axis] * inv_freq[pair][None, :]
    cos = jnp.concatenate([jnp.ones((n_ctx, LANES), F32), jnp.cos(ang)], axis=0)
    sin = jnp.concatenate([jnp.zeros((n_ctx, LANES), F32), jnp.sin(ang) * sign[None, :]], axis=0)
    return cos, sin


def kernel(x, c, ctx, c_ctx, ada_w, ada_b, norm1_w, norm2_w, ssd_in_w, ssd_conv_w, ssd_conv_b, ssd_dt_bias,
           ssd_a_log, ssd_d, ssd_norm_w, ssd_out_w, attn_qkv_w, attn_sink, attn_out_w, router_w, router_bias,
           moe_w_gate, moe_w_up, moe_w_down, shared_w_gate, shared_w_up, shared_w_down, final_norm_w):
    bsz, seq, d = x.shape
    n_ctx = ctx.shape[1]
    assert n_ctx == TILE and seq % TILE == 0 and d == D_MODEL and bsz < MOD_ROWS
    tpb = (n_ctx + seq) // TILE
    geo = {
        "batch": bsz,
        "tpb": tpb,
        "nch": (n_ctx + seq) // CHUNK,
        "ncc": n_ctx // CHUNK,
        "mod_row": lambda i: jnp.where(i % tpb == 0, bsz, i // tpb),
    }
    nt = bsz * (n_ctx + seq)
    h = jnp.concatenate([ctx, x], axis=1).reshape(nt, d)

    cc = jnp.zeros((MOD_ROWS, d), F32).at[:bsz].set(c).at[bsz].set(c_ctx)
    mod = _ada(cc, ada_w, ada_b).reshape(DEPTH, MOD_ROWS, 1, 6 * d)
    cos, sin = _rope_tables(n_ctx, seq)
    router_wt = router_w.T

    geo_lat = dict(geo, mod_row=lambda i: i // (tpb - 1))
    for l in range(DEPTH):
        j = l // 2
        last = l == DEPTH - 1
        if l % 2 == 0:
            w_in = ssd_in_w[j].astype(BF16)
            w_z = w_in[:, :D_INNER]
            w_xbc = w_in[:, D_INNER:D_INNER + CONV_DIM]
            w_dt = jnp.zeros((d, LANES), BF16).at[:, :2 * SSM_HEADS].set(w_in[:, D_INNER + CONV_DIM:])
            z, xbc, dt_raw = _norm_mod_matmul(h, norm1_w[l], mod[l], 0, [w_z, w_xbc, w_dt], [BF16, BF16, F32], geo)
            yn = _ssd_mixer(z, xbc, dt_raw, ssd_conv_w[j], ssd_conv_b[j], ssd_dt_bias[j], ssd_a_log[j],
                            ssd_d[j], ssd_norm_w[j], geo)
            h = _matmul_residual(yn, ssd_out_w[j].astype(BF16), h, mod[l], 2, geo, latent_only=last)
        else:
            w_qkv = attn_qkv_w[j].astype(BF16)
            qd = N_Q_HEADS * HEAD_DIM
            kd = N_KV_HEADS * HEAD_DIM
            dup = lambda w: jnp.repeat(w.reshape(d, N_KV_HEADS, 1, HEAD_DIM), 2, axis=2).reshape(d, 2 * kd)
            w_q = w_qkv[:, :qd]
            w_k = dup(w_qkv[:, qd:qd + kd])
            w_v = dup(w_qkv[:, qd + kd:])
            qr, kdup, vdup = _norm_mod_matmul(h, norm1_w[l], mod[l], 0, [w_q, w_k, w_v], [BF16, BF16, BF16], geo,
                                              rope=(True, True, False), scales=(HEAD_DIM ** -0.5, 1.0, 1.0),
                                              tables=(cos, sin))
            o = _attention(qr, kdup, vdup, attn_sink[j], geo, latent_only=last)
            h = _matmul_residual(o, attn_out_w[j].astype(BF16), h, mod[l], 2, geo, latent_only=last)
        h = _moe(h, norm2_w[l], mod[l], router_wt, router_bias, l, moe_w_gate, moe_w_up, moe_w_down,
                 shared_w_gate[l].astype(BF16), shared_w_up[l].astype(BF16), shared_w_down[l].astype(BF16),
                 geo_lat if last else geo)

    return _final_norm(h, final_norm_w).reshape(bsz, seq, d)
```

```python
import functools

import jax
import jax.numpy as jnp
import numpy as np
from jax import lax
from jax.experimental import pallas as pl
from jax.experimental.pallas import tpu as pltpu

F32 = jnp.float32
BF16 = jnp.bfloat16

D_MODEL = 1024
DEPTH = 4
EPS = 1e-6
GRID_W = 64

D_INNER = 2048
SSM_HEADDIM = 64
SSM_HEADS = 32
SSM_STATE = 128
SSM_GROUPS = 4
SSM_CONV = 5
GN = SSM_GROUPS * SSM_STATE
CONV_DIM = D_INNER + 2 * GN
CHUNK = 128

HEAD_DIM = 64
N_Q_HEADS = 16
N_KV_HEADS = 4
ROPE_BASE = 10000.0
ROPE_PAIRS = 16
WINDOW = 128

N_EXPERTS = 16
N_EXPERT_GROUPS = 4
EXPERTS_PER_GROUP = 4
D_EXPERT = 512

TILE = 256
LANES = 128
MOD_ROWS = 16
NEG = -1e30
VMEM_LIMIT = 56 * 1024 * 1024


def _cp(n_axes, sem="arbitrary"):
    return pltpu.CompilerParams(dimension_semantics=(sem,) * n_axes, vmem_limit_bytes=VMEM_LIMIT)


def _dot(a, b):
    return jnp.dot(a, b, preferred_element_type=F32)


def _dot_nt(a, b):
    return lax.dot_general(a, b, (((1,), (1,)), ((), ())), preferred_element_type=F32)


def _split_bf16(v):
    hi = v.astype(BF16)
    lo = (v - hi.astype(F32)).astype(BF16)
    return hi, lo


def _sigmoid(x):
    return 1.0 / (1.0 + jnp.exp(-x))


def _silu(x):
    return x * _sigmoid(x)


def _softplus(x):
    return jnp.maximum(x, 0.0) + jnp.log(1.0 + jnp.exp(-jnp.abs(x)))


def _rms_mod(x, nw, shift, scale):
    y = x * lax.rsqrt(jnp.mean(x * x, axis=-1, keepdims=True) + EPS)
    return (y * nw) * (1.0 + scale) + shift


def _ada_kernel(c_ref, w_ref, b_ref, o_ref):
    s = _silu(c_ref[...])
    o_ref[...] = jnp.dot(s, w_ref[...], preferred_element_type=F32,
                         precision=lax.Precision.HIGHEST) + b_ref[...]


def _ada(cc, ada_w, ada_b):
    depth, d, n = ada_w.shape
    tn = 1024
    return pl.pallas_call(
        _ada_kernel,
        grid=(depth, n // tn),
        in_specs=[
            pl.BlockSpec((MOD_ROWS, d), lambda l, j: (0, 0)),
            pl.BlockSpec((None, d, tn), lambda l, j: (l, 0, j)),
            pl.BlockSpec((None, 1, tn), lambda l, j: (l, 0, j)),
        ],
        out_specs=pl.BlockSpec((None, MOD_ROWS, tn), lambda l, j: (l, 0, j)),
        out_shape=jax.ShapeDtypeStruct((depth, MOD_ROWS, n), F32),
        compiler_params=_cp(2),
        name="ada_table",
    )(cc, ada_w, ada_b.reshape(depth, 1, n))


def _rope128(x, cos, sin_signed, low_half):
    partner = jnp.where(low_half, pltpu.roll(x, LANES - ROPE_PAIRS, axis=1), pltpu.roll(x, ROPE_PAIRS, axis=1))
    return x * cos + partner * sin_signed


def _nmm_kernel(*refs, n_out, rope, scales):
    x_ref, nw_ref, sh_ref, sc_ref = refs[:4]
    w_refs = refs[4:4 + n_out]
    pos = 4 + n_out
    if any(rope):
        cos_ref, sin_ref = refs[pos:pos + 2]
        pos += 2
    o_refs = refs[pos:pos + n_out]
    a = _rms_mod(x_ref[...], nw_ref[...], sh_ref[...], sc_ref[...]).astype(BF16)
    if any(rope):
        lane = lax.broadcasted_iota(jnp.int32, (TILE, LANES), 1)
        low_half = (lane % (2 * ROPE_PAIRS)) < ROPE_PAIRS
        cos = cos_ref[...]
        sin = sin_ref[...]
    for k in range(n_out):
        if rope[k]:
            n = o_refs[k].shape[1]
            for j in range(n // LANES):
                acc = _dot(a, w_refs[k][:, j * LANES:(j + 1) * LANES])
                acc = _rope128(acc, cos, sin, low_half) * scales[k]
                o_refs[k][:, j * LANES:(j + 1) * LANES] = acc.astype(o_refs[k].dtype)
        else:
            o_refs[k][...] = (_dot(a, w_refs[k][...]) * scales[k]).astype(o_refs[k].dtype)


def _norm_mod_matmul(x, nw, mod, part, weights, out_dtypes, geo, rope=None, scales=None, tables=None):
    nt, d = x.shape
    n_out = len(weights)
    rope = tuple(rope) if rope is not None else (False,) * n_out
    scales = tuple(scales) if scales is not None else (1.0,) * n_out
    row = geo["mod_row"]
    in_specs = [
        pl.BlockSpec((TILE, d), lambda i: (i, 0)),
        pl.BlockSpec((1, d), lambda i: (0, 0)),
        pl.BlockSpec((None, 1, d), lambda i: (row(i), 0, part)),
        pl.BlockSpec((None, 1, d), lambda i: (row(i), 0, part + 1)),
    ]
    args = [x, nw.reshape(1, d), mod, mod]
    for w in weights:
        in_specs.append(pl.BlockSpec(w.shape, lambda i: (0, 0)))
        args.append(w)
    if any(rope):
        tpb = geo["tpb"]
        in_specs += [pl.BlockSpec((TILE, LANES), lambda i: (i % tpb, 0))] * 2
        args += list(tables)
    return pl.pallas_call(
        functools.partial(_nmm_kernel, n_out=n_out, rope=rope, scales=scales),
        grid=(nt // TILE,),
        in_specs=in_specs,
        out_specs=[pl.BlockSpec((TILE, w.shape[1]), lambda i: (i, 0)) for w in weights],
        out_shape=[jax.ShapeDtypeStruct((nt, w.shape[1]), dt) for w, dt in zip(weights, out_dtypes)],
        compiler_params=_cp(1, "parallel"),
        name="norm_proj_rope" if any(rope) else "norm_proj",
    )(*args)


def _mmres_kernel(a_ref, w_ref, h_ref, g_ref, o_ref):
    o_ref[...] = h_ref[...] + g_ref[...] * _dot(a_ref[...], w_ref[...])


def _matmul_residual(a, w, h, mod, part, geo, latent_only=False):
    nt, k = a.shape
    d = w.shape[1]
    if latent_only:
        tpb, lat = geo["tpb"], geo["tpb"] - 1
        n_tiles = geo["batch"] * lat
        src = lambda i: (i // lat) * tpb + 1 + i % lat
        row = lambda i: i // lat
    else:
        n_tiles = nt // TILE
        src = lambda i: i
        row = geo["mod_row"]
    return pl.pallas_call(
        _mmres_kernel,
        grid=(n_tiles,),
        in_specs=[
            pl.BlockSpec((TILE, k), lambda i: (src(i), 0)),
            pl.BlockSpec((k, d), lambda i: (0, 0)),
            pl.BlockSpec((TILE, d), lambda i: (src(i), 0)),
            pl.BlockSpec((None, 1, d), lambda i: (row(i), 0, part)),
        ],
        out_specs=pl.BlockSpec((TILE, d), lambda i: (i, 0)),
        out_shape=jax.ShapeDtypeStruct((n_tiles * TILE, d), F32),
        compiler_params=_cp(1, "parallel"),
        name="proj_residual",
    )(a, w, h, mod)


def _ssd_chunk(x, bm, cm, dt, a_row, e_ref, state_ref, y_ref, direction):
    q = CHUNK
    ii = lax.broadcasted_iota(jnp.int32, (q, q), 0)
    jj = lax.broadcasted_iota(jnp.int32, (q, q), 1)
    tri = (jj <= ii) if direction == 0 else (jj >= ii)
    tri_b = jnp.where(tri, 1.0, 0.0).astype(BF16)
    da = dt * a_row
    da_hi, da_lo = _split_bf16(da)
    cum = _dot(tri_b, da_hi) + _dot(tri_b, da_lo)
    cum_t = cum.T
    e = e_ref[...]

    def expand(v):
        hi, lo = _split_bf16(v)
        return _dot(hi, e) + _dot(lo, e)

    dt_x = expand(dt)
    cum_x = expand(cum)
    edge = q - 1 if direction == 0 else 0
    tot_x = cum_x[edge:edge + 1, :]
    dtx = x * dt_x
    dtx_b = dtx.astype(BF16)
    to_end = (dtx * jnp.exp(tot_x - cum_x)).astype(BF16)
    from_start = jnp.exp(cum_x)
    chunk_decay = jnp.exp(tot_x)
    lane = lax.broadcasted_iota(jnp.int32, (q, LANES), 1)
    first_head = lane < SSM_HEADDIM
    gw = D_INNER // SSM_GROUPS
    for g in range(SSM_GROUPS):
        bg = bm[:, g * SSM_STATE:(g + 1) * SSM_STATE]
        cg = cm[:, g * SSM_STATE:(g + 1) * SSM_STATE]
        cb = _dot_nt(cg, bg)
        h_prev = state_ref[g]
        y_off = _dot(cg, h_prev.astype(BF16)) * from_start[:, g * gw:(g + 1) * gw]
        bg_t = bg.astype(F32).T.astype(BF16)
        state_ref[g] = h_prev * chunk_decay[:, g * gw:(g + 1) * gw] + _dot(bg_t, to_end[:, g * gw:(g + 1) * gw])
        for p in range(gw // LANES):
            ms = []
            for k in range(2):
                hl = 32 * direction + g * 8 + 2 * p + k
                seg = cum[:, hl:hl + 1] - cum_t[hl:hl + 1, :]
                decay = jnp.exp(jnp.where(tri, seg, NEG))
                ms.append((cb * decay).astype(BF16))
            lo = g * gw + p * LANES
            out2 = _dot(jnp.concatenate(ms, axis=0), dtx_b[:, lo:lo + LANES])
            y_diag = jnp.where(first_head, out2[:q], out2[q:])
            y_ref[:, lo:lo + LANES] = (y_diag + y_off[:, p * LANES:(p + 1) * LANES]).astype(y_ref.dtype)


def _ssd_bwd_kernel(cur_ref, prev_ref, next_ref, dt_ref, cw_ref, cbias_ref, dtb_ref, alog_ref, e_ref,
                    act_ref, yb_ref, state_ref, *, ncc, nch):
    s = pl.program_id(1)
    c = jnp.where(s < ncc, ncc - 1 - s, nch - 1 - (s - ncc))

    @pl.when(s == 0)
    def _():
        state_ref[...] = jnp.zeros_like(state_ref)

    has_prev = jnp.logical_and(c != 0, c != ncc)
    has_next = jnp.logical_and(c != ncc - 1, c != nch - 1)
    halo = prev_ref.shape[0]
    pad = (SSM_CONV - 1) // 2
    taps = [k for k in range(SSM_CONV) if k != pad]
    n_src = CHUNK + 2 * halo
    src = jnp.concatenate([prev_ref[...], cur_ref[...], next_ref[...]], axis=0)
    rr = lax.broadcasted_iota(jnp.int32, (len(taps) * CHUNK, n_src), 0)
    cc = lax.broadcasted_iota(jnp.int32, (len(taps) * CHUNK, n_src), 1)
    want = rr + (halo - pad)
    for n, k in enumerate(taps):
        want = jnp.where(rr >= n * CHUNK, rr - n * CHUNK + (halo - pad + k), want)
    inside = jnp.logical_and(jnp.logical_or(cc >= halo, has_prev), jnp.logical_or(cc < halo + CHUNK, has_next))
    shift = jnp.where(jnp.logical_and(cc == want, inside), 1.0, 0.0).astype(BF16)
    shifted = _dot(shift, src)
    acc = cbias_ref[...] + cw_ref[pad:pad + 1, :] * cur_ref[...].astype(F32)
    for n, k in enumerate(taps):
        acc = acc + cw_ref[k:k + 1, :] * shifted[n * CHUNK:(n + 1) * CHUNK]
    act = _silu(acc)
    act_b = act.astype(BF16)
    act_ref[...] = act_b

    dt = _softplus(dt_ref[...] + dtb_ref[...])
    a_row = -jnp.exp(alog_ref[...])
    _ssd_chunk(act[:, :D_INNER], act_b[:, D_INNER:D_INNER + GN], act_b[:, D_INNER + GN:], dt, a_row,
               e_ref, state_ref, yb_ref, 1)


def _ssd_fwd_kernel(act_ref, dt_ref, yb_ref, z_ref, dtb_ref, alog_ref, dskip_ref, nw_ref, e_ref,
                    o_ref, yf_ref, state_ref):
    s = pl.program_id(1)

    @pl.when(s == 0)
    def _():
        state_ref[...] = jnp.zeros_like(state_ref)

    act_b = act_ref[...]
    x = act_b[:, :D_INNER].astype(F32)
    dt = _softplus(dt_ref[...] + dtb_ref[...])
    a_row = -jnp.exp(alog_ref[...])
    _ssd_chunk(x, act_b[:, D_INNER:D_INNER + GN], act_b[:, D_INNER + GN:], dt, a_row, e_ref, state_ref, yf_ref, 0)
    y = yf_ref[...] + yb_ref[...] + x * dskip_ref[...]
    y = y * _silu(z_ref[...].astype(F32))
    y = y * lax.rsqrt(jnp.mean(y * y, axis=-1, keepdims=True) + EPS)
    o_ref[...] = (y * nw_ref[...]).astype(BF16)


def _ssd_mixer(z, xbc, dt_raw, conv_w, conv_b, dt_bias, a_log, d_skip, norm_w, geo):
    nt = z.shape[0]
    bsz, nch, ncc = geo["batch"], geo["nch"], geo["ncc"]
    halo = 16
    hb = CHUNK // halo
    n_halo_blocks = nt // halo

    def chunk_bwd(b, s):
        return b * nch + jnp.where(s < ncc, ncc - 1 - s, nch - 1 - (s - ncc))

    def chunk_fwd(b, s):
        return b * nch + s

    cw = jnp.zeros((8, CONV_DIM), F32).at[:SSM_CONV].set(conv_w)
    dtb = jnp.zeros((1, LANES), F32).at[0, :2 * SSM_HEADS].set(dt_bias.reshape(-1))
    alog = jnp.zeros((1, LANES), F32).at[0, :2 * SSM_HEADS].set(a_log.reshape(-1))
    heads = np.arange(D_INNER) // SSM_HEADDIM
    e_np = np.zeros((2, LANES, D_INNER), np.float32)
    for d in range(2):
        e_np[d, 32 * d + heads, np.arange(D_INNER)] = 1.0
    e_mats = jnp.asarray(e_np, BF16)
    const = lambda shape: pl.BlockSpec(shape, lambda b, s: (0,) * len(shape))

    act, yb = pl.pallas_call(
        functools.partial(_ssd_bwd_kernel, ncc=ncc, nch=nch),
        grid=(bsz, nch),
        in_specs=[
            pl.BlockSpec((CHUNK, CONV_DIM), lambda b, s: (chunk_bwd(b, s), 0)),
            pl.BlockSpec((halo, CONV_DIM), lambda b, s: (jnp.maximum(chunk_bwd(b, s) * hb - 1, 0), 0)),
            pl.BlockSpec((halo, CONV_DIM), lambda b, s: (jnp.minimum((chunk_bwd(b, s) + 1) * hb, n_halo_blocks - 1), 0)),
            pl.BlockSpec((CHUNK, LANES), lambda b, s: (chunk_bwd(b, s), 0)),
            const((8, CONV_DIM)),
            const((1, CONV_DIM)),
            const((1, LANES)),
            const((1, LANES)),
            pl.BlockSpec((None, LANES, D_INNER), lambda b, s: (1, 0, 0)),
        ],
        out_specs=[
            pl.BlockSpec((CHUNK, CONV_DIM), lambda b, s: (chunk_bwd(b, s), 0)),
            pl.BlockSpec((CHUNK, D_INNER), lambda b, s: (chunk_bwd(b, s), 0)),
        ],
        out_shape=[jax.ShapeDtypeStruct((nt, CONV_DIM), BF16), jax.ShapeDtypeStruct((nt, D_INNER), BF16)],
        scratch_shapes=[pltpu.VMEM((SSM_GROUPS, SSM_STATE, D_INNER // SSM_GROUPS), F32)],
        compiler_params=_cp(2),
        name="ssd_backward_pass",
    )(xbc, xbc, xbc, dt_raw, cw, conv_b.reshape(1, CONV_DIM), dtb, alog, e_mats)

    return pl.pallas_call(
        _ssd_fwd_kernel,
        grid=(bsz, nch),
        in_specs=[
            pl.BlockSpec((CHUNK, CONV_DIM), lambda b, s: (chunk_fwd(b, s), 0)),
            pl.BlockSpec((CHUNK, LANES), lambda b, s: (chunk_fwd(b, s), 0)),
            pl.BlockSpec((CHUNK, D_INNER), lambda b, s: (chunk_fwd(b, s), 0)),
            pl.BlockSpec((CHUNK, D_INNER), lambda b, s: (chunk_fwd(b, s), 0)),
            const((1, LANES)),
            const((1, LANES)),
            const((1, D_INNER)),
            const((1, D_INNER)),
            pl.BlockSpec((None, LANES, D_INNER), lambda b, s: (0, 0, 0)),
        ],
        out_specs=pl.BlockSpec((CHUNK, D_INNER), lambda b, s: (chunk_fwd(b, s), 0)),
        out_shape=jax.ShapeDtypeStruct((nt, D_INNER), BF16),
        scratch_shapes=[
            pltpu.VMEM((CHUNK, D_INNER), F32),
            pltpu.VMEM((SSM_GROUPS, SSM_STATE, D_INNER // SSM_GROUPS), F32),
        ],
        compiler_params=_cp(2),
        name="ssd_forward_pass",
    )(act, dt_raw, yb, z, dtb, alog, jnp.repeat(d_skip, SSM_HEADDIM).reshape(1, D_INNER),
      norm_w.reshape(1, D_INNER), e_mats)


def _attn_kernel(sink_ref, q_ref, kc_ref, vc_ref, kp_ref, ko_ref, kn_ref, vp_ref, vo_ref, vn_ref, o_ref,
                 *, ncc, n_lat_blocks, first):
    blk = pl.program_id(1) + first
    n = blk - ncc
    is_lat = n >= 0
    q = CHUNK
    ii = lax.broadcasted_iota(jnp.int32, (q, q), 0)
    jj = lax.broadcasted_iota(jnp.int32, (q, q), 1)
    ok_prev = jnp.logical_and(jj >= ii, jnp.logical_and(is_lat, n >= 1))
    ok_own = jnp.logical_and(jj >= 0, is_lat)
    ok_next = jnp.logical_and(jj <= ii, jnp.logical_and(is_lat, n <= n_lat_blocks - 2))
    n_ctx = kc_ref.shape[0]
    lane = lax.broadcasted_iota(jnp.int32, (q, LANES), 1)
    low = lane < HEAD_DIM
    zero = jnp.zeros((q, LANES), BF16)
    for kh in range(N_KV_HEADS):
        ks = slice(kh * LANES, (kh + 1) * LANES)
        kcat = jnp.concatenate([kc_ref[:, ks], kp_ref[:, ks], ko_ref[:, ks], kn_ref[:, ks]], axis=0)
        vcat = jnp.concatenate([vc_ref[:, ks], vp_ref[:, ks], vo_ref[:, ks], vn_ref[:, ks]], axis=0)
        lhs = []
        for m in range(2):
            qp = q_ref[:, (2 * kh + m) * LANES:(2 * kh + m + 1) * LANES]
            lhs += [jnp.where(low, qp, zero), jnp.where(low, zero, qp)]
        s_all = _dot_nt(jnp.concatenate(lhs, axis=0), kcat)
        ps, inv = [], []
        for gi in range(4):
            sink = sink_ref[kh * 4 + gi]
            sh = s_all[gi * q:(gi + 1) * q]
            sc = jnp.concatenate([sh[:, :n_ctx],
                                  jnp.where(ok_prev, sh[:, n_ctx:n_ctx + q], NEG),
                                  jnp.where(ok_own, sh[:, n_ctx + q:n_ctx + 2 * q], NEG),
                                  jnp.where(ok_next, sh[:, n_ctx + 2 * q:], NEG)], axis=1)
            mx = jnp.maximum(jnp.max(sc, axis=-1, keepdims=True), sink)
            p = jnp.exp(sc - mx)
            denom = jnp.sum(p, axis=-1, keepdims=True) + jnp.exp(sink - mx)
            ps.append(p.astype(BF16))
            inv.append(1.0 / denom)
        r = _dot(jnp.concatenate(ps, axis=0), vcat)
        for m in range(2):
            o = jnp.where(low, r[(2 * m) * q:(2 * m + 1) * q] * inv[2 * m],
                          r[(2 * m + 1) * q:(2 * m + 2) * q] * inv[2 * m + 1])
            o_ref[:, (2 * kh + m) * LANES:(2 * kh + m + 1) * LANES] = o.astype(BF16)


def _attention(qr, kd, vd, sink, geo, latent_only=False):
    nt = qr.shape[0]
    bsz, nch, ncc = geo["batch"], geo["nch"], geo["ncc"]
    nlb = nch - ncc
    kvw = kd.shape[1]
    ctx_rows = ncc * CHUNK
    first = ncc if latent_only else 0

    def win(o):
        return lambda b, j, *_: (b * nch + ncc + jnp.clip(j + first - ncc + o - 1, 0, nlb - 1), 0)

    grid_spec = pltpu.PrefetchScalarGridSpec(
        num_scalar_prefetch=1,
        grid=(bsz, nch - first),
        in_specs=[
            pl.BlockSpec((CHUNK, qr.shape[1]), lambda b, j, *_: (b * nch + first + j, 0)),
            pl.BlockSpec((ctx_rows, kvw), lambda b, j, *_: (b * (nch // ncc), 0)),
            pl.BlockSpec((ctx_rows, kvw), lambda b, j, *_: (b * (nch // ncc), 0)),
            pl.BlockSpec((CHUNK, kvw), win(0)),
            pl.BlockSpec((CHUNK, kvw), win(1)),
            pl.BlockSpec((CHUNK, kvw), win(2)),
            pl.BlockSpec((CHUNK, kvw), win(0)),
            pl.BlockSpec((CHUNK, kvw), win(1)),
            pl.BlockSpec((CHUNK, kvw), win(2)),
        ],
        out_specs=pl.BlockSpec((CHUNK, qr.shape[1]), lambda b, j, *_: (b * nch + first + j, 0)),
    )
    return pl.pallas_call(
        functools.partial(_attn_kernel, ncc=ncc, n_lat_blocks=nlb, first=first),
        grid_spec=grid_spec,
        out_shape=jax.ShapeDtypeStruct((nt, qr.shape[1]), BF16),
        compiler_params=_cp(2, "parallel"),
        name="window_attention",
    )(sink, qr, kd, vd, kd, kd, kd, vd, vd, vd)


RUN_ALIGN = 8
SORTED_ROWS = 2 * TILE + N_EXPERTS * RUN_ALIGN
RUN_BITS = 6


def _pack_bf16_pair(lo, hi):
    ulo = lax.bitcast_convert_type(lo, jnp.uint32)
    uhi = lax.bitcast_convert_type(hi, jnp.uint32)
    return (uhi & jnp.uint32(0xFFFF0000)) | (ulo >> 16)


def _unpack_bf16_pair(p):
    lo = lax.bitcast_convert_type(p << 16, F32)
    hi = lax.bitcast_convert_type(p & jnp.uint32(0xFFFF0000), F32)
    return lo.astype(BF16), hi.astype(BF16)


def _run_copies(meta, make_copy, wait):
    for e in range(N_EXPERTS):
        units = meta(e, 0) // RUN_ALIGN
        local = meta(e, 1)
        glob = meta(e, 2)
        for b in range(RUN_BITS - 1, -1, -1):
            size = RUN_ALIGN << b
            done = ((units >> (b + 1)) << (b + 1)) * RUN_ALIGN

            @pl.when(((units >> b) & 1) == 1)
            def _():
                cp = make_copy(pl.multiple_of(local + done, RUN_ALIGN), pl.multiple_of(glob + done, RUN_ALIGN), size)
                if wait:
                    cp.wait()
                else:
                    cp.start()


def _route_kernel(h_ref, nw_ref, sh_ref, sc_ref, rw_ref, rb_ref, wg_ref, wu_ref, wd_ref, xs_ref,
                  shared_ref, wcol_ref, meta_ref, cnt_ref,
                  sorted_ref, meta_v_ref, meta_s_ref, carry_ref, sem, *, cap):
    i = pl.program_id(0)
    last = pl.num_programs(0) - 1
    slot = i % 2
    t = TILE

    @pl.when(i == 0)
    def _():
        carry_ref[...] = jnp.zeros_like(carry_ref)

    m = _rms_mod(h_ref[...], nw_ref[...], sh_ref[...], sc_ref[...])
    mb, m_lo = _split_bf16(m)
    rw_hi = rw_ref[0]
    logits = (_dot(mb, rw_hi) + _dot(m_lo, rw_hi) + _dot(mb, rw_ref[1])).T[0:N_EXPERTS, :]
    hid = (_silu(_dot(mb, wg_ref[...])) * _dot(mb, wu_ref[...])).astype(BF16)
    shared_ref[...] = _dot(hid, wd_ref[...])
    scores = _sigmoid(logits)
    biased = scores + rb_ref[...]
    rows = [biased[e:e + 1, :] for e in range(N_EXPERTS)]
    srows = [scores[e:e + 1, :] for e in range(N_EXPERTS)]
    gscore = []
    for g in range(N_EXPERT_GROUPS):
        r = rows[g * 4:(g + 1) * 4]
        best = None
        for a in range(4):
            for b in range(a + 1, 4):
                pair = r[a] + r[b]
                best = pair if best is None else jnp.maximum(best, pair)
        gscore.append(best)
    gbest = jnp.maximum(jnp.maximum(gscore[0], gscore[1]), jnp.maximum(gscore[2], gscore[3]))
    gsel = jnp.full((1, t), N_EXPERT_GROUPS - 1, jnp.int32)
    for g in range(N_EXPERT_GROUPS - 2, -1, -1):
        gsel = jnp.where(gscore[g] == gbest, g, gsel)
    cand = [jnp.where(gsel == e // 4, rows[e], NEG) for e in range(N_EXPERTS)]
    best1 = functools.reduce(jnp.maximum, cand)
    e1 = jnp.full((1, t), N_EXPERTS - 1, jnp.int32)
    for e in range(N_EXPERTS - 2, -1, -1):
        e1 = jnp.where(cand[e] == best1, e, e1)
    cand2 = [jnp.where(e1 == e, NEG, cand[e]) for e in range(N_EXPERTS)]
    best2 = functools.reduce(jnp.maximum, cand2)
    e2 = jnp.full((1, t), N_EXPERTS - 1, jnp.int32)
    for e in range(N_EXPERTS - 2, -1, -1):
        e2 = jnp.where(cand2[e] == best2, e, e2)
    s1 = functools.reduce(jnp.add, [jnp.where(e1 == e, srows[e], 0.0) for e in range(N_EXPERTS)])
    s2 = functools.reduce(jnp.add, [jnp.where(e2 == e, srows[e], 0.0) for e in range(N_EXPERTS)])
    wsum = s1 + s2
    w1 = s1 / wsum
    w2 = s2 / wsum
    eid = lax.broadcasted_iota(jnp.int32, (N_EXPERTS, t), 0)
    oh1 = eid == e1
    oh2 = eid == e2
    onehot = jnp.where(jnp.logical_or(oh1, oh2), 1.0, 0.0)
    jr = lax.broadcasted_iota(jnp.int32, (t, t), 0)
    jc = lax.broadcasted_iota(jnp.int32, (t, t), 1)
    before = jnp.where(jr < jc, 1.0, 0.0).astype(BF16)
    prefix = _dot(onehot.astype(BF16), before)
    n_e = jnp.sum(onehot, axis=1, keepdims=True)
    run = jnp.floor((n_e + (RUN_ALIGN - 1)) * (1.0 / RUN_ALIGN)) * RUN_ALIGN
    run_b = jnp.broadcast_to(run, (N_EXPERTS, LANES))
    er = lax.broadcasted_iota(jnp.int32, (N_EXPERTS, N_EXPERTS), 0)
    ec = lax.broadcasted_iota(jnp.int32, (N_EXPERTS, N_EXPERTS), 1)
    lower = jnp.where(ec < er, 1.0, 0.0).astype(BF16)
    local = _dot(lower, run_b.astype(BF16))
    eid_l = lax.broadcasted_iota(jnp.int32, (N_EXPERTS, LANES), 0)
    glob = (eid_l * cap).astype(F32) + carry_ref[...]
    carry_ref[...] = carry_ref[...] + run_b
    rl1 = jnp.sum(jnp.where(oh1, prefix + local[:, 0:1], 0.0), axis=0, keepdims=True)
    rl2 = jnp.sum(jnp.where(oh2, prefix + local[:, 0:1], 0.0), axis=0, keepdims=True)
    lane_l = lax.broadcasted_iota(jnp.int32, (N_EXPERTS, LANES), 1)
    meta = jnp.where(lane_l == 0, run_b, jnp.where(lane_l == 1, local, jnp.where(lane_l == 2, glob, 0.0)))
    meta_ref[...] = meta.astype(jnp.int32)
    meta_v_ref[slot] = meta.astype(jnp.int32)
    cnt_ref[...] = carry_ref[...].astype(jnp.int32)
    wrows = jnp.concatenate([w1, w2, rl1, rl2, jnp.zeros((LANES - 4, t), F32)], axis=0)
    wcol_ref[...] = wrows.T

    meta_cp = pltpu.make_async_copy(meta_v_ref.at[slot], meta_s_ref.at[slot], sem.at[2 + slot])
    meta_cp.start()
    rr = lax.broadcasted_iota(jnp.int32, (SORTED_ROWS, t), 0)
    perm = jnp.where(jnp.logical_or(rr == rl1.astype(jnp.int32), rr == rl2.astype(jnp.int32)), 1.0, 0.0)
    srt = _dot(perm.astype(BF16), mb)
    half = sorted_ref.shape[2]
    sorted_ref[slot] = _pack_bf16_pair(srt[:, :half], srt[:, half:])

    def runs(s, wait):
        def make_copy(local_row, global_row, size):
            return pltpu.make_async_copy(sorted_ref.at[s, pl.ds(local_row, size)],
                                         xs_ref.at[pl.ds(global_row, size)], sem.at[s])
        _run_copies(lambda e, k: meta_s_ref[s, e, k], make_copy, wait)

    @pl.when(i > 0)
    def _():
        runs(1 - slot, True)

    meta_cp.wait()
    runs(slot, False)

    @pl.when(i == last)
    def _():
        runs(slot, True)


def _expert_kernel(blk_ref, exp_ref, nv_ref, new_ref, x_ref, wg_ref, wu_ref, wd_ref, y_ref, wgb_ref, wub_ref, wdb_ref):
    j = pl.program_id(0)
    nv = nv_ref[j]

    @pl.when(new_ref[j] == 1)
    def _():
        wgb_ref[...] = wg_ref[...].astype(BF16)
        wub_ref[...] = wu_ref[...].astype(BF16)
        wdb_ref[...] = wd_ref[...].astype(BF16)

    @pl.when(nv > 0)
    def _():
        rows = lax.broadcasted_iota(jnp.int32, (TILE, 1), 0)
        x_lo, x_hi = _unpack_bf16_pair(jnp.where(rows < nv, x_ref[...], jnp.uint32(0)))
        half = x_ref.shape[1]
        gate = _dot(x_lo, wgb_ref[0:half, :]) + _dot(x_hi, wgb_ref[half:, :])
        up = _dot(x_lo, wub_ref[0:half, :]) + _dot(x_hi, wub_ref[half:, :])
        hid = (_silu(gate) * up).astype(BF16)
        y = _dot(hid, wdb_ref[...]).astype(BF16).astype(F32)
        y_ref[...] = _pack_bf16_pair(y[:, :half], y[:, half:])

    @pl.when(nv <= 0)
    def _():
        y_ref[...] = jnp.zeros_like(y_ref)


def _combine_kernel(h_ref, shared_ref, g_ref, wcol_ref, meta0_ref, meta1_ref, meta2_ref, ys_ref, o_ref,
                    buf_ref, meta_v_ref, meta_s_ref, sem):
    t = TILE
    i = pl.program_id(0)
    n = pl.num_programs(0)
    n_gather_sems = 2

    def table_copy(s):
        return pltpu.make_async_copy(meta_v_ref.at[s], meta_s_ref.at[s], sem.at[n_gather_sems + s])

    def gather(table, slot, wait):
        def make_copy(local_row, global_row, size):
            return pltpu.make_async_copy(ys_ref.at[pl.ds(global_row, size)],
                                         buf_ref.at[slot, pl.ds(local_row, size)], sem.at[slot])
        _run_copies(lambda e, k: meta_s_ref[table, e, k], make_copy, wait)

    @pl.when(i == 0)
    def _():
        meta_v_ref[0] = meta0_ref[...]
        table_copy(0).start()
        table_copy(0).wait()
        gather(0, 0, False)
        meta_v_ref[1] = meta1_ref[...]
        table_copy(1).start()

    @pl.when(i + 1 < n)
    def _():
        table_copy((i + 1) % 3).wait()
        gather((i + 1) % 3, (i + 1) % 2, False)

    @pl.when(i + 2 < n)
    def _():
        meta_v_ref[(i + 2) % 3] = meta2_ref[...]
        table_copy((i + 2) % 3).start()

    cur = i % 3
    gather(cur, i % 2, True)
    total = meta_s_ref[cur, N_EXPERTS - 1, 0] + meta_s_ref[cur, N_EXPERTS - 1, 1]
    rows = lax.broadcasted_iota(jnp.int32, (SORTED_ROWS, 1), 0)
    y_lo, y_hi = _unpack_bf16_pair(jnp.where(rows < total, buf_ref[i % 2], jnp.uint32(0)))
    w = wcol_ref[...]
    cols = lax.broadcasted_iota(jnp.int32, (t, SORTED_ROWS), 1)
    pick1 = jnp.where(cols == w[:, 2:3].astype(jnp.int32), 1.0, 0.0).astype(BF16)
    pick2 = jnp.where(cols == w[:, 3:4].astype(jnp.int32), 1.0, 0.0).astype(BF16)
    routed = jnp.concatenate([w[:, 0:1] * _dot(pick1, y) + w[:, 1:2] * _dot(pick2, y) for y in (y_lo, y_hi)], axis=1)
    o_ref[...] = h_ref[...] + g_ref[...] * (shared_ref[...] + routed)


def _moe(h, nw, mod, router_hl, router_b, layer, wg, wu, wd, swg, swu, swd, geo):
    nt, d = h.shape
    ntiles = nt // TILE
    row = geo["mod_row"]
    cap = -(-(nt + ntiles * (RUN_ALIGN - 1)) // TILE) * TILE
    dump_blk = N_EXPERTS * cap // TILE
    n_rows = N_EXPERTS * cap + TILE
    const = lambda shape: pl.BlockSpec(shape, lambda i: (0,) * len(shape))

    xs, shared, wcol, meta, cnt = pl.pallas_call(
        functools.partial(_route_kernel, cap=cap),
        grid=(ntiles,),
        in_specs=[
            pl.BlockSpec((TILE, d), lambda i: (i, 0)),
            const((1, d)),
            pl.BlockSpec((None, 1, d), lambda i: (row(i), 0, 3)),
            pl.BlockSpec((None, 1, d), lambda i: (row(i), 0, 4)),
            const((2, d, LANES)),
            const((N_EXPERTS, 1)),
            const(swg.shape), const(swu.shape), const(swd.shape),
        ],
        out_specs=[
            pl.BlockSpec(memory_space=pl.ANY),
            pl.BlockSpec((TILE, d), lambda i: (i, 0)),
            pl.BlockSpec((TILE, LANES), lambda i: (i, 0)),
            pl.BlockSpec((None, N_EXPERTS, LANES), lambda i: (i, 0, 0)),
            const((N_EXPERTS, LANES)),
        ],
        out_shape=[
            jax.ShapeDtypeStruct((n_rows, d // 2), jnp.uint32),
            jax.ShapeDtypeStruct((nt, d), F32),
            jax.ShapeDtypeStruct((nt, LANES), F32),
            jax.ShapeDtypeStruct((ntiles, N_EXPERTS, LANES), jnp.int32),
            jax.ShapeDtypeStruct((N_EXPERTS, LANES), jnp.int32),
        ],
        scratch_shapes=[
            pltpu.VMEM((2, SORTED_ROWS, d // 2), jnp.uint32),
            pltpu.VMEM((2, N_EXPERTS, LANES), jnp.int32),
            pltpu.SMEM((2, N_EXPERTS, LANES), jnp.int32),
            pltpu.VMEM((N_EXPERTS, LANES), F32),
            pltpu.SemaphoreType.DMA((4,)),
        ],
        compiler_params=_cp(1),
        name="moe_route",
    )(h, nw.reshape(1, d), mod, mod, router_hl, router_b.reshape(N_EXPERTS, 1), swg, swu, swd)

    counts = cnt[:, 0]
    tiles_e = (counts + TILE - 1) // TILE
    ends = jnp.cumsum(tiles_e)
    starts = ends - tiles_e
    n_sched = -(-(2 * nt + ntiles * N_EXPERTS * (RUN_ALIGN - 1)) // TILE) + N_EXPERTS
    jidx = jnp.arange(n_sched, dtype=jnp.int32)
    e_of = jnp.minimum(jnp.sum((jidx[:, None] >= ends[None, :]).astype(jnp.int32), axis=1), N_EXPERTS - 1)
    local = jidx - starts[e_of]
    active = jidx < ends[-1]
    last_e = e_of[jnp.maximum(ends[-1] - 1, 0)]
    tile_blk = jnp.where(active, e_of * (cap // TILE) + local, dump_blk).astype(jnp.int32)
    tile_e = jnp.where(active, e_of, last_e).astype(jnp.int32)
    tile_nv = jnp.where(active, jnp.minimum(counts[e_of] - local * TILE, TILE), 0).astype(jnp.int32)
    tile_new = jnp.logical_and(active, local == 0).astype(jnp.int32)

    ys = pl.pallas_call(
        _expert_kernel,
        grid_spec=pltpu.PrefetchScalarGridSpec(
            num_scalar_prefetch=4,
            grid=(n_sched,),
            in_specs=[
                pl.BlockSpec((TILE, d // 2), lambda j, blk, ex, nv, new: (blk[j], 0)),
                pl.BlockSpec((None, None, d, D_EXPERT), lambda j, blk, ex, nv, new: (layer, ex[j], 0, 0)),
                pl.BlockSpec((None, None, d, D_EXPERT), lambda j, blk, ex, nv, new: (layer, ex[j], 0, 0)),
                pl.BlockSpec((None, None, D_EXPERT, d), lambda j, blk, ex, nv, new: (layer, ex[j], 0, 0)),
            ],
            out_specs=pl.BlockSpec((TILE, d // 2), lambda j, blk, ex, nv, new: (blk[j], 0)),
            scratch_shapes=[
                pltpu.VMEM((d, D_EXPERT), BF16),
                pltpu.VMEM((d, D_EXPERT), BF16),
                pltpu.VMEM((D_EXPERT, d), BF16),
            ],
        ),
        out_shape=jax.ShapeDtypeStruct((n_rows, d // 2), jnp.uint32),
        compiler_params=_cp(1),
        name="moe_experts",
    )(tile_blk, tile_e, tile_nv, tile_new, xs, wg, wu, wd)

    return pl.pallas_call(
        _combine_kernel,
        grid=(ntiles,),
        in_specs=[
            pl.BlockSpec((TILE, d), lambda i: (i, 0)),
            pl.BlockSpec((TILE, d), lambda i: (i, 0)),
            pl.BlockSpec((None, 1, d), lambda i: (row(i), 0, 5)),
            pl.BlockSpec((TILE, LANES), lambda i: (i, 0)),
            pl.BlockSpec((None, N_EXPERTS, LANES), lambda i: (i, 0, 0)),
            pl.BlockSpec((None, N_EXPERTS, LANES), lambda i: (jnp.minimum(i + 1, ntiles - 1), 0, 0)),
            pl.BlockSpec((None, N_EXPERTS, LANES), lambda i: (jnp.minimum(i + 2, ntiles - 1), 0, 0)),
            pl.BlockSpec(memory_space=pl.ANY),
        ],
        out_specs=pl.BlockSpec((TILE, d), lambda i: (i, 0)),
        out_shape=jax.ShapeDtypeStruct((nt, d), F32),
        scratch_shapes=[
            pltpu.VMEM((2, SORTED_ROWS, d // 2), jnp.uint32),
            pltpu.VMEM((3, N_EXPERTS, LANES), jnp.int32),
            pltpu.SMEM((3, N_EXPERTS, LANES), jnp.int32),
            pltpu.SemaphoreType.DMA((5,)),
        ],
        compiler_params=_cp(1),
        name="moe_combine",
    )(h, shared, mod, wcol, meta, meta, meta, ys)


def _final_kernel(h_ref, w_ref, o_ref):
    x = h_ref[...]
    o_ref[...] = x * lax.rsqrt(jnp.mean(x * x, axis=-1, keepdims=True) + EPS) * w_ref[...]


def _final_norm(h, w):
    n, d = h.shape
    rows = 2 * TILE
    return pl.pallas_call(
        _final_kernel,
        grid=(n // rows,),
        in_specs=[
            pl.BlockSpec((rows, d), lambda i: (i, 0)),
            pl.BlockSpec((1, d), lambda i: (0, 0)),
        ],
        out_specs=pl.BlockSpec((rows, d), lambda i: (i, 0)),
        out_shape=jax.ShapeDtypeStruct((n, d), F32),
        compiler_params=_cp(1, "parallel"),
        name="final_norm",
    )(h, w.reshape(1, d))


def _rope_tables(n_ctx, seq):
    lane = np.arange(LANES) % HEAD_DIM
    axis = lane // (2 * ROPE_PAIRS)
    pair = lane % ROPE_PAIRS
    sign = np.where((lane % (2 * ROPE_PAIRS)) < ROPE_PAIRS, -1.0, 1.0).astype(np.float32)
    inv_freq = ROPE_BASE ** (-jnp.arange(ROPE_PAIRS, dtype=F32) / ROPE_PAIRS)
    t = jnp.arange(seq)
    posn = jnp.stack([t // GRID_W, t % GRID_W], axis=-1).astype(F32)
    ang = posn[:, axis] * inv_freq[pair][None, :]
    cos = jnp.concatenate([jnp.ones((n_ctx, LANES), F32), jnp.cos(ang)], axis=0)
    sin = jnp.concatenate([jnp.zeros((n_ctx, LANES), F32), jnp.sin(ang) * sign[None, :]], axis=0)
    return cos, sin


def kernel(x, c, ctx, c_ctx, ada_w, ada_b, norm1_w, norm2_w, ssd_in_w, ssd_conv_w, ssd_conv_b, ssd_dt_bias,
           ssd_a_log, ssd_d, ssd_norm_w, ssd_out_w, attn_qkv_w, attn_sink, attn_out_w, router_w, router_bias,
           moe_w_gate, moe_w_up, moe_w_down, shared_w_gate, shared_w_up, shared_w_down, final_norm_w):
    bsz, seq, d = x.shape
    n_ctx = ctx.shape[1]
    assert n_ctx == TILE and seq % TILE == 0 and d == D_MODEL and bsz < MOD_ROWS
    tpb = (n_ctx + seq) // TILE
    geo = {
        "batch": bsz,
        "tpb": tpb,
        "nch": (n_ctx + seq) // CHUNK,
        "ncc": n_ctx // CHUNK,
        "mod_row": lambda i: jnp.where(i % tpb == 0, bsz, i // tpb),
    }
    nt = bsz * (n_ctx + seq)
    h = jnp.concatenate([ctx, x], axis=1).reshape(nt, d)

    cc = jnp.zeros((MOD_ROWS, d), F32).at[:bsz].set(c).at[bsz].set(c_ctx)
    mod = _ada(cc, ada_w, ada_b).reshape(DEPTH, MOD_ROWS, 1, 6 * d)
    cos, sin = _rope_tables(n_ctx, seq)
    rw_pad = jnp.zeros((d, LANES), F32).at[:, :N_EXPERTS].set(router_w)
    rw_hi = rw_pad.astype(BF16)
    router_hl = jnp.stack([rw_hi, (rw_pad - rw_hi.astype(F32)).astype(BF16)])

    geo_lat = dict(geo, mod_row=lambda i: i // (tpb - 1))
    for l in range(DEPTH):
        j = l // 2
        last = l == DEPTH - 1
        if l % 2 == 0:
            w_in = ssd_in_w[j].astype(BF16)
            w_z = w_in[:, :D_INNER]
            w_xbc = w_in[:, D_INNER:D_INNER + CONV_DIM]
            w_dt = jnp.zeros((d, LANES), BF16).at[:, :2 * SSM_HEADS].set(w_in[:, D_INNER + CONV_DIM:])
            z, xbc, dt_raw = _norm_mod_matmul(h, norm1_w[l], mod[l], 0, [w_z, w_xbc, w_dt], [BF16, BF16, F32], geo)
            yn = _ssd_mixer(z, xbc, dt_raw, ssd_conv_w[j], ssd_conv_b[j], ssd_dt_bias[j], ssd_a_log[j],
                            ssd_d[j], ssd_norm_w[j], geo)
            h = _matmul_residual(yn, ssd_out_w[j].astype(BF16), h, mod[l], 2, geo, latent_only=last)
        else:
            w_qkv = attn_qkv_w[j].astype(BF16)
            qd = N_Q_HEADS * HEAD_DIM
            kd = N_KV_HEADS * HEAD_DIM
            dup = lambda w: jnp.repeat(w.reshape(d, N_KV_HEADS, 1, HEAD_DIM), 2, axis=2).reshape(d, 2 * kd)
            w_q = w_qkv[:, :qd]
            w_k = dup(w_qkv[:, qd:qd + kd])
            w_v = dup(w_qkv[:, qd + kd:])
            qr, kdup, vdup = _norm_mod_matmul(h, norm1_w[l], mod[l], 0, [w_q, w_k, w_v], [BF16, BF16, BF16], geo,
                                              rope=(True, True, False), scales=(HEAD_DIM ** -0.5, 1.0, 1.0),
                                              tables=(cos, sin))
            o = _attention(qr, kdup, vdup, attn_sink[j], geo, latent_only=last)
            h = _matmul_residual(o, attn_out_w[j].astype(BF16), h, mod[l], 2, geo, latent_only=last)
        h = _moe(h, norm2_w[l], mod[l], router_hl, router_bias, l, moe_w_gate, moe_w_up, moe_w_down,
                 shared_w_gate[l].astype(BF16), shared_w_up[l].astype(BF16), shared_w_down[l].astype(BF16),
                 geo_lat if last else geo)

    return _final_norm(h, final_norm_w).reshape(bsz, seq, d)
```

```python
import functools

import jax
import jax.numpy as jnp
import numpy as np
from jax import lax
from jax.experimental import pallas as pl
from jax.experimental.pallas import tpu as pltpu

F32 = jnp.float32
BF16 = jnp.bfloat16

D_MODEL = 1024
DEPTH = 4
EPS = 1e-6
GRID_W = 64

D_INNER = 2048
SSM_HEADDIM = 64
SSM_HEADS = 32
SSM_STATE = 128
SSM_GROUPS = 4
SSM_CONV = 5
GN = SSM_GROUPS * SSM_STATE
CONV_DIM = D_INNER + 2 * GN
CHUNK = 128

HEAD_DIM = 64
N_Q_HEADS = 16
N_KV_HEADS = 4
ROPE_BASE = 10000.0
ROPE_PAIRS = 16
WINDOW = 128

N_EXPERTS = 16
N_EXPERT_GROUPS = 4
EXPERTS_PER_GROUP = 4
D_EXPERT = 512

TILE = 256
WIDE = 3
LANES = 128
MOD_ROWS = 16
NEG = -1e30
VMEM_LIMIT = 56 * 1024 * 1024


def _cp(n_axes, sem="arbitrary"):
    return pltpu.CompilerParams(dimension_semantics=(sem,) * n_axes, vmem_limit_bytes=VMEM_LIMIT)


def _dot(a, b):
    return jnp.dot(a, b, preferred_element_type=F32)


def _dot_nt(a, b):
    return lax.dot_general(a, b, (((1,), (1,)), ((), ())), preferred_element_type=F32)


def _split_bf16(v):
    hi = v.astype(BF16)
    lo = (v - hi.astype(F32)).astype(BF16)
    return hi, lo


def _sigmoid(x):
    return 1.0 / (1.0 + jnp.exp(-x))


def _silu(x):
    return x * _sigmoid(x)


def _softplus(x):
    return jnp.maximum(x, 0.0) + jnp.log(1.0 + jnp.exp(-jnp.abs(x)))


def _rms_mod(x, nw, shift, scale):
    y = x * lax.rsqrt(jnp.mean(x * x, axis=-1, keepdims=True) + EPS)
    return (y * nw) * (1.0 + scale) + shift


def _ada_kernel(c_ref, w_ref, b_ref, o_ref):
    s = _silu(c_ref[...])
    o_ref[...] = jnp.dot(s, w_ref[...], preferred_element_type=F32,
                         precision=lax.Precision.HIGHEST) + b_ref[...]


def _ada(cc, ada_w, ada_b):
    depth, d, n = ada_w.shape
    tn = 1024
    return pl.pallas_call(
        _ada_kernel,
        grid=(depth, n // tn),
        in_specs=[
            pl.BlockSpec((MOD_ROWS, d), lambda l, j: (0, 0)),
            pl.BlockSpec((None, d, tn), lambda l, j: (l, 0, j)),
            pl.BlockSpec((None, 1, tn), lambda l, j: (l, 0, j)),
        ],
        out_specs=pl.BlockSpec((None, MOD_ROWS, tn), lambda l, j: (l, 0, j)),
        out_shape=jax.ShapeDtypeStruct((depth, MOD_ROWS, n), F32),
        compiler_params=_cp(2),
        name="ada_table",
    )(cc, ada_w, ada_b.reshape(depth, 1, n))


def _rope128(x, cos, sin_signed, low_half):
    partner = jnp.where(low_half, pltpu.roll(x, LANES - ROPE_PAIRS, axis=1), pltpu.roll(x, ROPE_PAIRS, axis=1))
    return x * cos + partner * sin_signed


def _pick_ctx(step, rows, per_batch, ctx_val, batch_val):
    r = lax.broadcasted_iota(jnp.int32, (rows, 1), 0)
    is_ctx = jnp.logical_and(r < TILE, step % per_batch == 0)
    return jnp.where(is_ctx, ctx_val, batch_val)


def _nmm_kernel(*refs, n_out, rope, scales, per_batch):
    x_ref, nw_ref, sh_ref, sc_ref, shc_ref, scc_ref = refs[:6]
    w_refs = refs[6:6 + n_out]
    pos = 6 + n_out
    if any(rope):
        cos_ref, sin_ref = refs[pos:pos + 2]
        pos += 2
    o_refs = refs[pos:pos + n_out]
    step = pl.program_id(0)
    rows = x_ref.shape[0]
    shift = _pick_ctx(step, rows, per_batch, shc_ref[...], sh_ref[...])
    scale = _pick_ctx(step, rows, per_batch, scc_ref[...], sc_ref[...])
    a = _rms_mod(x_ref[...], nw_ref[...], shift, scale).astype(BF16)
    if any(rope):
        lane = lax.broadcasted_iota(jnp.int32, (rows, LANES), 1)
        low_half = (lane % (2 * ROPE_PAIRS)) < ROPE_PAIRS
        cos = cos_ref[...]
        sin = sin_ref[...]
    for k in range(n_out):
        if rope[k]:
            n = o_refs[k].shape[1]
            for j in range(n // LANES):
                acc = _dot(a, w_refs[k][:, j * LANES:(j + 1) * LANES])
                acc = _rope128(acc, cos, sin, low_half) * scales[k]
                o_refs[k][:, j * LANES:(j + 1) * LANES] = acc.astype(o_refs[k].dtype)
        else:
            o_refs[k][...] = (_dot(a, w_refs[k][...]) * scales[k]).astype(o_refs[k].dtype)


def _norm_mod_matmul(x, nw, mod, part, weights, out_dtypes, geo, rope=None, scales=None, tables=None):
    nt, d = x.shape
    n_out = len(weights)
    rope = tuple(rope) if rope is not None else (False,) * n_out
    scales = tuple(scales) if scales is not None else (1.0,) * n_out
    rows = WIDE * TILE
    per_batch = geo["tpb"] // WIDE
    bsz = geo["batch"]
    batch_mod = lambda p: pl.BlockSpec((None, 1, d), lambda i: (i // per_batch, 0, p))
    ctx_mod = lambda p: pl.BlockSpec((None, 1, d), lambda i: (bsz, 0, p))
    in_specs = [
        pl.BlockSpec((rows, d), lambda i: (i, 0)),
        pl.BlockSpec((1, d), lambda i: (0, 0)),
        batch_mod(part), batch_mod(part + 1), ctx_mod(part), ctx_mod(part + 1),
    ]
    args = [x, nw.reshape(1, d), mod, mod, mod, mod]
    for w in weights:
        in_specs.append(pl.BlockSpec(w.shape, lambda i: (0, 0), pipeline_mode=pl.Buffered(1)))
        args.append(w)
    if any(rope):
        in_specs += [pl.BlockSpec((rows, LANES), lambda i: (i % per_batch, 0))] * 2
        args += list(tables)
    return pl.pallas_call(
        functools.partial(_nmm_kernel, n_out=n_out, rope=rope, scales=scales, per_batch=per_batch),
        grid=(nt // rows,),
        in_specs=in_specs,
        out_specs=[pl.BlockSpec((rows, w.shape[1]), lambda i: (i, 0)) for w in weights],
        out_shape=[jax.ShapeDtypeStruct((nt, w.shape[1]), dt) for w, dt in zip(weights, out_dtypes)],
        compiler_params=_cp(1, "parallel"),
        name="norm_proj_rope" if any(rope) else "norm_proj",
    )(*args)


def _mmres_kernel(a_ref, w_ref, h_ref, g_ref, gc_ref, o_ref, *, per_batch):
    gate = g_ref[...]
    if per_batch is not None:
        gate = _pick_ctx(pl.program_id(0), a_ref.shape[0], per_batch, gc_ref[...], gate)
    o_ref[...] = h_ref[...] + gate * _dot(a_ref[...], w_ref[...])


def _matmul_residual(a, w, h, mod, part, geo, latent_only=False):
    nt, k = a.shape
    d = w.shape[1]
    bsz = geo["batch"]
    if latent_only:
        tpb, lat = geo["tpb"], geo["tpb"] - 1
        rows, n_tiles, per_batch = TILE, bsz * lat, None
        src = lambda i: (i // lat) * tpb + 1 + i % lat
        row = lambda i: i // lat
    else:
        rows = WIDE * TILE
        per_batch = geo["tpb"] // WIDE
        n_tiles = nt // rows
        src = lambda i: i
        row = lambda i: i // per_batch
    return pl.pallas_call(
        functools.partial(_mmres_kernel, per_batch=per_batch),
        grid=(n_tiles,),
        in_specs=[
            pl.BlockSpec((rows, k), lambda i: (src(i), 0)),
            pl.BlockSpec((k, d), lambda i: (0, 0), pipeline_mode=pl.Buffered(1)),
            pl.BlockSpec((rows, d), lambda i: (src(i), 0)),
            pl.BlockSpec((None, 1, d), lambda i: (row(i), 0, part)),
            pl.BlockSpec((None, 1, d), lambda i: (bsz, 0, part)),
        ],
        out_specs=pl.BlockSpec((rows, d), lambda i: (i, 0)),
        out_shape=jax.ShapeDtypeStruct((n_tiles * rows, d), F32),
        compiler_params=_cp(1, "parallel"),
        name="proj_residual",
    )(a, w, h, mod, mod)


def _ssd_chunk(x, bm, cm, dt, a_row, e_ref, state_ref, y_ref, direction):
    q = CHUNK
    ii = lax.broadcasted_iota(jnp.int32, (q, q), 0)
    jj = lax.broadcasted_iota(jnp.int32, (q, q), 1)
    tri = (jj <= ii) if direction == 0 else (jj >= ii)
    tri_b = jnp.where(tri, 1.0, 0.0).astype(BF16)
    da = dt * a_row
    da_hi, da_lo = _split_bf16(da)
    cum = _dot(tri_b, da_hi) + _dot(tri_b, da_lo)
    cum_t = cum.T
    dt_parts = _split_bf16(dt)
    cum_parts = _split_bf16(cum)
    edge = q - 1 if direction == 0 else 0
    lane = lax.broadcasted_iota(jnp.int32, (q, LANES), 1)
    first_head = lane < SSM_HEADDIM
    gw = D_INNER // SSM_GROUPS
    for g in range(SSM_GROUPS):
        e_g = e_ref[:, g * gw:(g + 1) * gw]
        dt_x = _dot(dt_parts[0], e_g) + _dot(dt_parts[1], e_g)
        cum_x = _dot(cum_parts[0], e_g) + _dot(cum_parts[1], e_g)
        tot_x = cum_x[edge:edge + 1, :]
        dtx = x[:, g * gw:(g + 1) * gw] * dt_x
        dtx_b = dtx.astype(BF16)
        to_end = (dtx * jnp.exp(tot_x - cum_x)).astype(BF16)
        bg = bm[:, g * SSM_STATE:(g + 1) * SSM_STATE]
        cg = cm[:, g * SSM_STATE:(g + 1) * SSM_STATE]
        cb = _dot_nt(cg, bg)
        h_prev = state_ref[g]
        y_off = _dot(cg, h_prev.astype(BF16)) * jnp.exp(cum_x)
        bg_t = bg.astype(F32).T.astype(BF16)
        state_ref[g] = h_prev * jnp.exp(tot_x) + _dot(bg_t, to_end)
        for p in range(gw // LANES):
            ms = []
            for k in range(2):
                hl = 32 * direction + g * 8 + 2 * p + k
                seg = cum[:, hl:hl + 1] - cum_t[hl:hl + 1, :]
                decay = jnp.exp(jnp.where(tri, seg, NEG))
                ms.append((cb * decay).astype(BF16))
            lo = g * gw + p * LANES
            out2 = _dot(jnp.concatenate(ms, axis=0), dtx_b[:, p * LANES:(p + 1) * LANES])
            y_diag = jnp.where(first_head, out2[:q], out2[q:])
            y_ref[:, lo:lo + LANES] = (y_diag + y_off[:, p * LANES:(p + 1) * LANES]).astype(y_ref.dtype)


def _ssd_bwd_kernel(cur_ref, prev_ref, next_ref, dt_ref, cw_ref, cbias_ref, dtb_ref, alog_ref, e_ref,
                    act_ref, yb_ref, state_ref, *, ncc, nch):
    s = pl.program_id(1)
    c = jnp.where(s < ncc, ncc - 1 - s, nch - 1 - (s - ncc))

    @pl.when(s == 0)
    def _():
        state_ref[...] = jnp.zeros_like(state_ref)

    has_prev = jnp.logical_and(c != 0, c != ncc)
    has_next = jnp.logical_and(c != ncc - 1, c != nch - 1)
    halo = prev_ref.shape[0]
    pad = (SSM_CONV - 1) // 2
    taps = [k for k in range(SSM_CONV) if k != pad]
    n_src = CHUNK + 2 * halo
    src = jnp.concatenate([prev_ref[...], cur_ref[...], next_ref[...]], axis=0)
    rr = lax.broadcasted_iota(jnp.int32, (len(taps) * CHUNK, n_src), 0)
    cc = lax.broadcasted_iota(jnp.int32, (len(taps) * CHUNK, n_src), 1)
    want = rr + (halo - pad)
    for n, k in enumerate(taps):
        want = jnp.where(rr >= n * CHUNK, rr - n * CHUNK + (halo - pad + k), want)
    inside = jnp.logical_and(jnp.logical_or(cc >= halo, has_prev), jnp.logical_or(cc < halo + CHUNK, has_next))
    shift = jnp.where(jnp.logical_and(cc == want, inside), 1.0, 0.0).astype(BF16)
    shifted = _dot(shift, src)
    acc = cbias_ref[...] + cw_ref[pad:pad + 1, :] * cur_ref[...].astype(F32)
    for n, k in enumerate(taps):
        acc = acc + cw_ref[k:k + 1, :] * shifted[n * CHUNK:(n + 1) * CHUNK]
    act = _silu(acc)
    act_b = act.astype(BF16)
    act_ref[...] = act_b

    dt = _softplus(dt_ref[...] + dtb_ref[...])
    a_row = -jnp.exp(alog_ref[...])
    _ssd_chunk(act[:, :D_INNER], act_b[:, D_INNER:D_INNER + GN], act_b[:, D_INNER + GN:], dt, a_row,
               e_ref, state_ref, yb_ref, 1)


def _ssd_fwd_kernel(act_ref, dt_ref, yb_ref, z_ref, dtb_ref, alog_ref, dskip_ref, nw_ref, e_ref,
                    o_ref, yf_ref, state_ref):
    s = pl.program_id(1)

    @pl.when(s == 0)
    def _():
        state_ref[...] = jnp.zeros_like(state_ref)

    act_b = act_ref[...]
    x = act_b[:, :D_INNER].astype(F32)
    dt = _softplus(dt_ref[...] + dtb_ref[...])
    a_row = -jnp.exp(alog_ref[...])
    _ssd_chunk(x, act_b[:, D_INNER:D_INNER + GN], act_b[:, D_INNER + GN:], dt, a_row, e_ref, state_ref, yf_ref, 0)
    y = yf_ref[...] + yb_ref[...] + x * dskip_ref[...]
    y = y * _silu(z_ref[...].astype(F32))
    y = y * lax.rsqrt(jnp.mean(y * y, axis=-1, keepdims=True) + EPS)
    o_ref[...] = (y * nw_ref[...]).astype(BF16)


def _ssd_mixer(z, xbc, dt_raw, conv_w, conv_b, dt_bias, a_log, d_skip, norm_w, geo):
    nt = z.shape[0]
    bsz, nch, ncc = geo["batch"], geo["nch"], geo["ncc"]
    halo = 16
    hb = CHUNK // halo
    n_halo_blocks = nt // halo

    def chunk_bwd(b, s):
        return b * nch + jnp.where(s < ncc, ncc - 1 - s, nch - 1 - (s - ncc))

    def chunk_fwd(b, s):
        return b * nch + s

    cw = jnp.zeros((8, CONV_DIM), F32).at[:SSM_CONV].set(conv_w)
    dtb = jnp.zeros((1, LANES), F32).at[0, :2 * SSM_HEADS].set(dt_bias.reshape(-1))
    alog = jnp.zeros((1, LANES), F32).at[0, :2 * SSM_HEADS].set(a_log.reshape(-1))
    heads = np.arange(D_INNER) // SSM_HEADDIM
    e_np = np.zeros((2, LANES, D_INNER), np.float32)
    for d in range(2):
        e_np[d, 32 * d + heads, np.arange(D_INNER)] = 1.0
    e_mats = jnp.asarray(e_np, BF16)
    const = lambda shape: pl.BlockSpec(shape, lambda b, s: (0,) * len(shape))

    act, yb = pl.pallas_call(
        functools.partial(_ssd_bwd_kernel, ncc=ncc, nch=nch),
        grid=(bsz, nch),
        in_specs=[
            pl.BlockSpec((CHUNK, CONV_DIM), lambda b, s: (chunk_bwd(b, s), 0)),
            pl.BlockSpec((halo, CONV_DIM), lambda b, s: (jnp.maximum(chunk_bwd(b, s) * hb - 1, 0), 0)),
            pl.BlockSpec((halo, CONV_DIM), lambda b, s: (jnp.minimum((chunk_bwd(b, s) + 1) * hb, n_halo_blocks - 1), 0)),
            pl.BlockSpec((CHUNK, LANES), lambda b, s: (chunk_bwd(b, s), 0)),
            const((8, CONV_DIM)),
            const((1, CONV_DIM)),
            const((1, LANES)),
            const((1, LANES)),
            pl.BlockSpec((None, LANES, D_INNER), lambda b, s: (1, 0, 0)),
        ],
        out_specs=[
            pl.BlockSpec((CHUNK, CONV_DIM), lambda b, s: (chunk_bwd(b, s), 0)),
            pl.BlockSpec((CHUNK, D_INNER), lambda b, s: (chunk_bwd(b, s), 0)),
        ],
        out_shape=[jax.ShapeDtypeStruct((nt, CONV_DIM), BF16), jax.ShapeDtypeStruct((nt, D_INNER), BF16)],
        scratch_shapes=[pltpu.VMEM((SSM_GROUPS, SSM_STATE, D_INNER // SSM_GROUPS), F32)],
        compiler_params=_cp(2),
        name="ssd_backward_pass",
    )(xbc, xbc, xbc, dt_raw, cw, conv_b.reshape(1, CONV_DIM), dtb, alog, e_mats)

    return pl.pallas_call(
        _ssd_fwd_kernel,
        grid=(bsz, nch),
        in_specs=[
            pl.BlockSpec((CHUNK, CONV_DIM), lambda b, s: (chunk_fwd(b, s), 0)),
            pl.BlockSpec((CHUNK, LANES), lambda b, s: (chunk_fwd(b, s), 0)),
            pl.BlockSpec((CHUNK, D_INNER), lambda b, s: (chunk_fwd(b, s), 0)),
            pl.BlockSpec((CHUNK, D_INNER), lambda b, s: (chunk_fwd(b, s), 0)),
            const((1, LANES)),
            const((1, LANES)),
            const((1, D_INNER)),
            const((1, D_INNER)),
            pl.BlockSpec((None, LANES, D_INNER), lambda b, s: (0, 0, 0)),
        ],
        out_specs=pl.BlockSpec((CHUNK, D_INNER), lambda b, s: (chunk_fwd(b, s), 0)),
        out_shape=jax.ShapeDtypeStruct((nt, D_INNER), BF16),
        scratch_shapes=[
            pltpu.VMEM((CHUNK, D_INNER), F32),
            pltpu.VMEM((SSM_GROUPS, SSM_STATE, D_INNER // SSM_GROUPS), F32),
        ],
        compiler_params=_cp(2),
        name="ssd_forward_pass",
    )(act, dt_raw, yb, z, dtb, alog, jnp.repeat(d_skip, SSM_HEADDIM).reshape(1, D_INNER),
      norm_w.reshape(1, D_INNER), e_mats)


def _attn_kernel(sink_ref, q_ref, kc_ref, vc_ref, kp_ref, ko_ref, kn_ref, vp_ref, vo_ref, vn_ref, o_ref,
                 *, ncc, n_lat_blocks, first):
    blk = pl.program_id(1) + first
    n = blk - ncc
    is_lat = n >= 0
    q = CHUNK
    ii = lax.broadcasted_iota(jnp.int32, (q, q), 0)
    jj = lax.broadcasted_iota(jnp.int32, (q, q), 1)
    ok_prev = jnp.logical_and(jj >= ii, jnp.logical_and(is_lat, n >= 1))
    ok_own = jnp.logical_and(jj >= 0, is_lat)
    ok_next = jnp.logical_and(jj <= ii, jnp.logical_and(is_lat, n <= n_lat_blocks - 2))
    n_ctx = kc_ref.shape[0]
    lane = lax.broadcasted_iota(jnp.int32, (q, LANES), 1)
    low = lane < HEAD_DIM
    zero = jnp.zeros((q, LANES), BF16)
    for kh in range(N_KV_HEADS):
        ks = slice(kh * LANES, (kh + 1) * LANES)
        kcat = jnp.concatenate([kc_ref[:, ks], kp_ref[:, ks], ko_ref[:, ks], kn_ref[:, ks]], axis=0)
        vcat = jnp.concatenate([vc_ref[:, ks], vp_ref[:, ks], vo_ref[:, ks], vn_ref[:, ks]], axis=0)
        lhs = []
        for m in range(2):
            qp = q_ref[:, (2 * kh + m) * LANES:(2 * kh + m + 1) * LANES]
            lhs += [jnp.where(low, qp, zero), jnp.where(low, zero, qp)]
        s_all = _dot_nt(jnp.concatenate(lhs, axis=0), kcat)
        ps, inv = [], []
        for gi in range(4):
            sink = sink_ref[kh * 4 + gi]
            sh = s_all[gi * q:(gi + 1) * q]
            sc = jnp.concatenate([sh[:, :n_ctx],
                                  jnp.where(ok_prev, sh[:, n_ctx:n_ctx + q], NEG),
                                  jnp.where(ok_own, sh[:, n_ctx + q:n_ctx + 2 * q], NEG),
                                  jnp.where(ok_next, sh[:, n_ctx + 2 * q:], NEG)], axis=1)
            mx = jnp.maximum(jnp.max(sc, axis=-1, keepdims=True), sink)
            p = jnp.exp(sc - mx)
            denom = jnp.sum(p, axis=-1, keepdims=True) + jnp.exp(sink - mx)
            ps.append(p.astype(BF16))
            inv.append(1.0 / denom)
        r = _dot(jnp.concatenate(ps, axis=0), vcat)
        for m in range(2):
            o = jnp.where(low, r[(2 * m) * q:(2 * m + 1) * q] * inv[2 * m],
                          r[(2 * m + 1) * q:(2 * m + 2) * q] * inv[2 * m + 1])
            o_ref[:, (2 * kh + m) * LANES:(2 * kh + m + 1) * LANES] = o.astype(BF16)


def _attention(qr, kd, vd, sink, geo, latent_only=False):
    nt = qr.shape[0]
    bsz, nch, ncc = geo["batch"], geo["nch"], geo["ncc"]
    nlb = nch - ncc
    kvw = kd.shape[1]
    ctx_rows = ncc * CHUNK
    first = ncc if latent_only else 0

    def win(o):
        return lambda b, j, *_: (b * nch + ncc + jnp.clip(j + first - ncc + o - 1, 0, nlb - 1), 0)

    grid_spec = pltpu.PrefetchScalarGridSpec(
        num_scalar_prefetch=1,
        grid=(bsz, nch - first),
        in_specs=[
            pl.BlockSpec((CHUNK, qr.shape[1]), lambda b, j, *_: (b * nch + first + j, 0)),
            pl.BlockSpec((ctx_rows, kvw), lambda b, j, *_: (b * (nch // ncc), 0)),
            pl.BlockSpec((ctx_rows, kvw), lambda b, j, *_: (b * (nch // ncc), 0)),
            pl.BlockSpec((CHUNK, kvw), win(0)),
            pl.BlockSpec((CHUNK, kvw), win(1)),
            pl.BlockSpec((CHUNK, kvw), win(2)),
            pl.BlockSpec((CHUNK, kvw), win(0)),
            pl.BlockSpec((CHUNK, kvw), win(1)),
            pl.BlockSpec((CHUNK, kvw), win(2)),
        ],
        out_specs=pl.BlockSpec((CHUNK, qr.shape[1]), lambda b, j, *_: (b * nch + first + j, 0)),
    )
    return pl.pallas_call(
        functools.partial(_attn_kernel, ncc=ncc, n_lat_blocks=nlb, first=first),
        grid_spec=grid_spec,
        out_shape=jax.ShapeDtypeStruct((nt, qr.shape[1]), BF16),
        compiler_params=_cp(2, "parallel"),
        name="window_attention",
    )(sink, qr, kd, vd, kd, kd, kd, vd, vd, vd)


RUN_ALIGN = 8
SORTED_ROWS = 2 * TILE + N_EXPERTS * RUN_ALIGN
RUN_BITS = 6


def _pack_bf16_pair(lo, hi):
    ulo = lax.bitcast_convert_type(lo, jnp.uint32)
    uhi = lax.bitcast_convert_type(hi, jnp.uint32)
    return (uhi & jnp.uint32(0xFFFF0000)) | (ulo >> 16)


def _unpack_bf16_pair(p):
    lo = lax.bitcast_convert_type(p << 16, F32)
    hi = lax.bitcast_convert_type(p & jnp.uint32(0xFFFF0000), F32)
    return lo.astype(BF16), hi.astype(BF16)


def _run_copies(meta, make_copy, wait):
    for e in range(N_EXPERTS):
        units = meta(e, 0) // RUN_ALIGN
        local = meta(e, 1)
        glob = meta(e, 2)
        for b in range(RUN_BITS - 1, -1, -1):
            size = RUN_ALIGN << b
            done = ((units >> (b + 1)) << (b + 1)) * RUN_ALIGN

            @pl.when(((units >> b) & 1) == 1)
            def _():
                cp = make_copy(pl.multiple_of(local + done, RUN_ALIGN), pl.multiple_of(glob + done, RUN_ALIGN), size)
                if wait:
                    cp.wait()
                else:
                    cp.start()


def _route_kernel(h_ref, nw_ref, sh_ref, sc_ref, rw_ref, rb_ref, wg_ref, wu_ref, wd_ref, xs_ref,
                  shared_ref, wcol_ref, meta_ref, cnt_ref,
                  sorted_ref, meta_v_ref, meta_s_ref, carry_ref, sem, *, cap):
    i = pl.program_id(0)
    last = pl.num_programs(0) - 1
    slot = i % 2
    t = TILE

    @pl.when(i == 0)
    def _():
        carry_ref[...] = jnp.zeros_like(carry_ref)

    m = _rms_mod(h_ref[...], nw_ref[...], sh_ref[...], sc_ref[...])
    mb, m_lo = _split_bf16(m)
    rw_hi = rw_ref[0]
    logits = (_dot(mb, rw_hi) + _dot(m_lo, rw_hi) + _dot(mb, rw_ref[1])).T[0:N_EXPERTS, :]
    hid = (_silu(_dot(mb, wg_ref[...])) * _dot(mb, wu_ref[...])).astype(BF16)
    shared_ref[...] = _dot(hid, wd_ref[...])
    scores = _sigmoid(logits)
    biased = scores + rb_ref[...]
    rows = [biased[e:e + 1, :] for e in range(N_EXPERTS)]
    srows = [scores[e:e + 1, :] for e in range(N_EXPERTS)]
    gscore = []
    for g in range(N_EXPERT_GROUPS):
        r = rows[g * 4:(g + 1) * 4]
        best = None
        for a in range(4):
            for b in range(a + 1, 4):
                pair = r[a] + r[b]
                best = pair if best is None else jnp.maximum(best, pair)
        gscore.append(best)
    gbest = jnp.maximum(jnp.maximum(gscore[0], gscore[1]), jnp.maximum(gscore[2], gscore[3]))
    gsel = jnp.full((1, t), N_EXPERT_GROUPS - 1, jnp.int32)
    for g in range(N_EXPERT_GROUPS - 2, -1, -1):
        gsel = jnp.where(gscore[g] == gbest, g, gsel)
    cand = [jnp.where(gsel == e // 4, rows[e], NEG) for e in range(N_EXPERTS)]
    best1 = functools.reduce(jnp.maximum, cand)
    e1 = jnp.full((1, t), N_EXPERTS - 1, jnp.int32)
    for e in range(N_EXPERTS - 2, -1, -1):
        e1 = jnp.where(cand[e] == best1, e, e1)
    cand2 = [jnp.where(e1 == e, NEG, cand[e]) for e in range(N_EXPERTS)]
    best2 = functools.reduce(jnp.maximum, cand2)
    e2 = jnp.full((1, t), N_EXPERTS - 1, jnp.int32)
    for e in range(N_EXPERTS - 2, -1, -1):
        e2 = jnp.where(cand2[e] == best2, e, e2)
    s1 = functools.reduce(jnp.add, [jnp.where(e1 == e, srows[e], 0.0) for e in range(N_EXPERTS)])
    s2 = functools.reduce(jnp.add, [jnp.where(e2 == e, srows[e], 0.0) for e in range(N_EXPERTS)])
    wsum = s1 + s2
    w1 = s1 / wsum
    w2 = s2 / wsum
    eid = lax.broadcasted_iota(jnp.int32, (N_EXPERTS, t), 0)
    oh1 = eid == e1
    oh2 = eid == e2
    onehot = jnp.where(jnp.logical_or(oh1, oh2), 1.0, 0.0)
    jr = lax.broadcasted_iota(jnp.int32, (t, t), 0)
    jc = lax.broadcasted_iota(jnp.int32, (t, t), 1)
    before = jnp.where(jr < jc, 1.0, 0.0).astype(BF16)
    prefix = _dot(onehot.astype(BF16), before)
    n_e = jnp.sum(onehot, axis=1, keepdims=True)
    run = jnp.floor((n_e + (RUN_ALIGN - 1)) * (1.0 / RUN_ALIGN)) * RUN_ALIGN
    run_b = jnp.broadcast_to(run, (N_EXPERTS, LANES))
    er = lax.broadcasted_iota(jnp.int32, (N_EXPERTS, N_EXPERTS), 0)
    ec = lax.broadcasted_iota(jnp.int32, (N_EXPERTS, N_EXPERTS), 1)
    lower = jnp.where(ec < er, 1.0, 0.0).astype(BF16)
    local = _dot(lower, run_b.astype(BF16))
    eid_l = lax.broadcasted_iota(jnp.int32, (N_EXPERTS, LANES), 0)
    glob = (eid_l * cap).astype(F32) + carry_ref[...]
    carry_ref[...] = carry_ref[...] + run_b
    rl1 = jnp.sum(jnp.where(oh1, prefix + local[:, 0:1], 0.0), axis=0, keepdims=True)
    rl2 = jnp.sum(jnp.where(oh2, prefix + local[:, 0:1], 0.0), axis=0, keepdims=True)
    lane_l = lax.broadcasted_iota(jnp.int32, (N_EXPERTS, LANES), 1)
    meta = jnp.where(lane_l == 0, run_b, jnp.where(lane_l == 1, local, jnp.where(lane_l == 2, glob, 0.0)))
    meta_ref[...] = meta.astype(jnp.int32)
    meta_v_ref[slot] = meta.astype(jnp.int32)
    cnt_ref[...] = carry_ref[...].astype(jnp.int32)
    wrows = jnp.concatenate([w1, w2, rl1, rl2, jnp.zeros((LANES - 4, t), F32)], axis=0)
    wcol_ref[...] = wrows.T

    meta_cp = pltpu.make_async_copy(meta_v_ref.at[slot], meta_s_ref.at[slot], sem.at[2 + slot])
    meta_cp.start()
    rr = lax.broadcasted_iota(jnp.int32, (SORTED_ROWS, t), 0)
    perm = jnp.where(jnp.logical_or(rr == rl1.astype(jnp.int32), rr == rl2.astype(jnp.int32)), 1.0, 0.0)
    srt = _dot(perm.astype(BF16), mb)
    half = sorted_ref.shape[2]
    sorted_ref[slot] = _pack_bf16_pair(srt[:, :half], srt[:, half:])

    def runs(s, wait):
        def make_copy(local_row, global_row, size):
            return pltpu.make_async_copy(sorted_ref.at[s, pl.ds(local_row, size)],
                                         xs_ref.at[pl.ds(global_row, size)], sem.at[s])
        _run_copies(lambda e, k: meta_s_ref[s, e, k], make_copy, wait)

    @pl.when(i > 0)
    def _():
        runs(1 - slot, True)

    meta_cp.wait()
    runs(slot, False)

    @pl.when(i == last)
    def _():
        runs(slot, True)


def _expert_kernel(blk_ref, exp_ref, nv_ref, new_ref, x_ref, wg_ref, wu_ref, wd_ref, y_ref, wgb_ref, wub_ref, wdb_ref):
    j = pl.program_id(0)
    nv = nv_ref[j]

    @pl.when(new_ref[j] == 1)
    def _():
        wgb_ref[...] = wg_ref[...].astype(BF16)
        wub_ref[...] = wu_ref[...].astype(BF16)
        wdb_ref[...] = wd_ref[...].astype(BF16)

    @pl.when(nv > 0)
    def _():
        rows = lax.broadcasted_iota(jnp.int32, (TILE, 1), 0)
        x_lo, x_hi = _unpack_bf16_pair(jnp.where(rows < nv, x_ref[...], jnp.uint32(0)))
        half = x_ref.shape[1]
        gate = _dot(x_lo, wgb_ref[0:half, :]) + _dot(x_hi, wgb_ref[half:, :])
        up = _dot(x_lo, wub_ref[0:half, :]) + _dot(x_hi, wub_ref[half:, :])
        hid = (_silu(gate) * up).astype(BF16)
        y = _dot(hid, wdb_ref[...]).astype(BF16).astype(F32)
        y_ref[...] = _pack_bf16_pair(y[:, :half], y[:, half:])

    @pl.when(nv <= 0)
    def _():
        y_ref[...] = jnp.zeros_like(y_ref)


def _combine_kernel(h_ref, shared_ref, g_ref, wcol_ref, meta0_ref, meta1_ref, meta2_ref, fw_ref, ys_ref, o_ref,
                    buf_ref, meta_v_ref, meta_s_ref, sem, *, final):
    t = TILE
    i = pl.program_id(0)
    n = pl.num_programs(0)
    n_gather_sems = 2

    def table_copy(s):
        return pltpu.make_async_copy(meta_v_ref.at[s], meta_s_ref.at[s], sem.at[n_gather_sems + s])

    def gather(table, slot, wait):
        def make_copy(local_row, global_row, size):
            return pltpu.make_async_copy(ys_ref.at[pl.ds(global_row, size)],
                                         buf_ref.at[slot, pl.ds(local_row, size)], sem.at[slot])
        _run_copies(lambda e, k: meta_s_ref[table, e, k], make_copy, wait)

    @pl.when(i == 0)
    def _():
        meta_v_ref[0] = meta0_ref[...]
        table_copy(0).start()
        table_copy(0).wait()
        gather(0, 0, False)
        meta_v_ref[1] = meta1_ref[...]
        table_copy(1).start()

    @pl.when(i + 1 < n)
    def _():
        table_copy((i + 1) % 3).wait()
        gather((i + 1) % 3, (i + 1) % 2, False)

    @pl.when(i + 2 < n)
    def _():
        meta_v_ref[(i + 2) % 3] = meta2_ref[...]
        table_copy((i + 2) % 3).start()

    cur = i % 3
    gather(cur, i % 2, True)
    total = meta_s_ref[cur, N_EXPERTS - 1, 0] + meta_s_ref[cur, N_EXPERTS - 1, 1]
    rows = lax.broadcasted_iota(jnp.int32, (SORTED_ROWS, 1), 0)
    y_lo, y_hi = _unpack_bf16_pair(jnp.where(rows < total, buf_ref[i % 2], jnp.uint32(0)))
    w = wcol_ref[...]
    cols = lax.broadcasted_iota(jnp.int32, (t, SORTED_ROWS), 1)
    pick1 = jnp.where(cols == w[:, 2:3].astype(jnp.int32), 1.0, 0.0).astype(BF16)
    pick2 = jnp.where(cols == w[:, 3:4].astype(jnp.int32), 1.0, 0.0).astype(BF16)
    routed = jnp.concatenate([w[:, 0:1] * _dot(pick1, y) + w[:, 1:2] * _dot(pick2, y) for y in (y_lo, y_hi)], axis=1)
    out = h_ref[...] + g_ref[...] * (shared_ref[...] + routed)
    if final:
        out = out * lax.rsqrt(jnp.mean(out * out, axis=-1, keepdims=True) + EPS) * fw_ref[...]
    o_ref[...] = out


def _moe(h, nw, mod, router_hl, router_b, layer, wg, wu, wd, swg, swu, swd, geo, final_w=None):
    nt, d = h.shape
    ntiles = nt // TILE
    row = geo["mod_row"]
    cap = -(-(nt + ntiles * (RUN_ALIGN - 1)) // TILE) * TILE
    dump_blk = N_EXPERTS * cap // TILE
    n_rows = N_EXPERTS * cap + TILE
    const = lambda shape: pl.BlockSpec(shape, lambda i: (0,) * len(shape))

    xs, shared, wcol, meta, cnt = pl.pallas_call(
        functools.partial(_route_kernel, cap=cap),
        grid=(ntiles,),
        in_specs=[
            pl.BlockSpec((TILE, d), lambda i: (i, 0)),
            const((1, d)),
            pl.BlockSpec((None, 1, d), lambda i: (row(i), 0, 3)),
            pl.BlockSpec((None, 1, d), lambda i: (row(i), 0, 4)),
            const((2, d, LANES)),
            const((N_EXPERTS, 1)),
            const(swg.shape), const(swu.shape), const(swd.shape),
        ],
        out_specs=[
            pl.BlockSpec(memory_space=pl.ANY),
            pl.BlockSpec((TILE, d), lambda i: (i, 0)),
            pl.BlockSpec((TILE, LANES), lambda i: (i, 0)),
            pl.BlockSpec((None, N_EXPERTS, LANES), lambda i: (i, 0, 0)),
            const((N_EXPERTS, LANES)),
        ],
        out_shape=[
            jax.ShapeDtypeStruct((n_rows, d // 2), jnp.uint32),
            jax.ShapeDtypeStruct((nt, d), F32),
            jax.ShapeDtypeStruct((nt, LANES), F32),
            jax.ShapeDtypeStruct((ntiles, N_EXPERTS, LANES), jnp.int32),
            jax.ShapeDtypeStruct((N_EXPERTS, LANES), jnp.int32),
        ],
        scratch_shapes=[
            pltpu.VMEM((2, SORTED_ROWS, d // 2), jnp.uint32),
            pltpu.VMEM((2, N_EXPERTS, LANES), jnp.int32),
            pltpu.SMEM((2, N_EXPERTS, LANES), jnp.int32),
            pltpu.VMEM((N_EXPERTS, LANES), F32),
            pltpu.SemaphoreType.DMA((4,)),
        ],
        compiler_params=_cp(1),
        name="moe_route",
    )(h, nw.reshape(1, d), mod, mod, router_hl, router_b.reshape(N_EXPERTS, 1), swg, swu, swd)

    counts = cnt[:, 0]
    tiles_e = (counts + TILE - 1) // TILE
    ends = jnp.cumsum(tiles_e)
    starts = ends - tiles_e
    n_sched = -(-(2 * nt + ntiles * N_EXPERTS * (RUN_ALIGN - 1)) // TILE) + N_EXPERTS
    jidx = jnp.arange(n_sched, dtype=jnp.int32)
    e_of = jnp.minimum(jnp.sum((jidx[:, None] >= ends[None, :]).astype(jnp.int32), axis=1), N_EXPERTS - 1)
    local = jidx - starts[e_of]
    active = jidx < ends[-1]
    last_e = e_of[jnp.maximum(ends[-1] - 1, 0)]
    tile_blk = jnp.where(active, e_of * (cap // TILE) + local, dump_blk).astype(jnp.int32)
    tile_e = jnp.where(active, e_of, last_e).astype(jnp.int32)
    tile_nv = jnp.where(active, jnp.minimum(counts[e_of] - local * TILE, TILE), 0).astype(jnp.int32)
    tile_new = jnp.logical_and(active, local == 0).astype(jnp.int32)

    ys = pl.pallas_call(
        _expert_kernel,
        grid_spec=pltpu.PrefetchScalarGridSpec(
            num_scalar_prefetch=4,
            grid=(n_sched,),
            in_specs=[
                pl.BlockSpec((TILE, d // 2), lambda j, blk, ex, nv, new: (blk[j], 0)),
                pl.BlockSpec((None, None, d, D_EXPERT), lambda j, blk, ex, nv, new: (layer, ex[j], 0, 0)),
                pl.BlockSpec((None, None, d, D_EXPERT), lambda j, blk, ex, nv, new: (layer, ex[j], 0, 0)),
                pl.BlockSpec((None, None, D_EXPERT, d), lambda j, blk, ex, nv, new: (layer, ex[j], 0, 0)),
            ],
            out_specs=pl.BlockSpec((TILE, d // 2), lambda j, blk, ex, nv, new: (blk[j], 0)),
            scratch_shapes=[
                pltpu.VMEM((d, D_EXPERT), BF16),
                pltpu.VMEM((d, D_EXPERT), BF16),
                pltpu.VMEM((D_EXPERT, d), BF16),
            ],
        ),
        out_shape=jax.ShapeDtypeStruct((n_rows, d // 2), jnp.uint32),
        compiler_params=_cp(1),
        name="moe_experts",
    )(tile_blk, tile_e, tile_nv, tile_new, xs, wg, wu, wd)

    final = final_w is not None
    fw = (final_w if final else jnp.ones((d,), F32)).reshape(1, d)
    return pl.pallas_call(
        functools.partial(_combine_kernel, final=final),
        grid=(ntiles,),
        in_specs=[
            pl.BlockSpec((TILE, d), lambda i: (i, 0)),
            pl.BlockSpec((TILE, d), lambda i: (i, 0)),
            pl.BlockSpec((None, 1, d), lambda i: (row(i), 0, 5)),
            pl.BlockSpec((TILE, LANES), lambda i: (i, 0)),
            pl.BlockSpec((None, N_EXPERTS, LANES), lambda i: (i, 0, 0)),
            pl.BlockSpec((None, N_EXPERTS, LANES), lambda i: (jnp.minimum(i + 1, ntiles - 1), 0, 0)),
            pl.BlockSpec((None, N_EXPERTS, LANES), lambda i: (jnp.minimum(i + 2, ntiles - 1), 0, 0)),
            pl.BlockSpec((1, d), lambda i: (0, 0)),
            pl.BlockSpec(memory_space=pl.ANY),
        ],
        out_specs=pl.BlockSpec((TILE, d), lambda i: (i, 0)),
        out_shape=jax.ShapeDtypeStruct((nt, d), F32),
        scratch_shapes=[
            pltpu.VMEM((2, SORTED_ROWS, d // 2), jnp.uint32),
            pltpu.VMEM((3, N_EXPERTS, LANES), jnp.int32),
            pltpu.SMEM((3, N_EXPERTS, LANES), jnp.int32),
            pltpu.SemaphoreType.DMA((5,)),
        ],
        compiler_params=_cp(1),
        name="moe_combine",
    )(h, shared, mod, wcol, meta, meta, meta, fw, ys)


def _rope_tables(n_ctx, seq):
    lane = np.arange(LANES) % HEAD_DIM
    axis = lane // (2 * ROPE_PAIRS)
    pair = lane % ROPE_PAIRS
    sign = np.where((lane % (2 * ROPE_PAIRS)) < ROPE_PAIRS, -1.0, 1.0).astype(np.float32)
    inv_freq = ROPE_BASE ** (-jnp.arange(ROPE_PAIRS, dtype=F32) / ROPE_PAIRS)
    t = jnp.arange(seq)
    posn = jnp.stack([t // GRID_W, t % GRID_W], axis=-1).astype(F32)
    ang = posn[:, axis] * inv_freq[pair][None, :]
    cos = jnp.concatenate([jnp.ones((n_ctx, LANES), F32), jnp.cos(ang)], axis=0)
    sin = jnp.concatenate([jnp.zeros((n_ctx, LANES), F32), jnp.sin(ang) * sign[None, :]], axis=0)
    return cos, sin


def kernel(x, c, ctx, c_ctx, ada_w, ada_b, norm1_w, norm2_w, ssd_in_w, ssd_conv_w, ssd_conv_b, ssd_dt_bias,
           ssd_a_log, ssd_d, ssd_norm_w, ssd_out_w, attn_qkv_w, attn_sink, attn_out_w, router_w, router_bias,
           moe_w_gate, moe_w_up, moe_w_down, shared_w_gate, shared_w_up, shared_w_down, final_norm_w):
    bsz, seq, d = x.shape
    n_ctx = ctx.shape[1]
    assert n_ctx == TILE and seq % TILE == 0 and d == D_MODEL and bsz < MOD_ROWS
    tpb = (n_ctx + seq) // TILE
    assert tpb % WIDE == 0
    geo = {
        "batch": bsz,
        "tpb": tpb,
        "nch": (n_ctx + seq) // CHUNK,
        "ncc": n_ctx // CHUNK,
        "mod_row": lambda i: jnp.where(i % tpb == 0, bsz, i // tpb),
    }
    nt = bsz * (n_ctx + seq)
    h = jnp.concatenate([ctx, x], axis=1).reshape(nt, d)

    cc = jnp.zeros((MOD_ROWS, d), F32).at[:bsz].set(c).at[bsz].set(c_ctx)
    mod = _ada(cc, ada_w, ada_b).reshape(DEPTH, MOD_ROWS, 1, 6 * d)
    cos, sin = _rope_tables(n_ctx, seq)
    rw_pad = jnp.zeros((d, LANES), F32).at[:, :N_EXPERTS].set(router_w)
    rw_hi = rw_pad.astype(BF16)
    router_hl = jnp.stack([rw_hi, (rw_pad - rw_hi.astype(F32)).astype(BF16)])

    geo_lat = dict(geo, mod_row=lambda i: i // (tpb - 1))
    for l in range(DEPTH):
        j = l // 2
        last = l == DEPTH - 1
        if l % 2 == 0:
            w_in = ssd_in_w[j].astype(BF16)
            w_z = w_in[:, :D_INNER]
            w_xbc = w_in[:, D_INNER:D_INNER + CONV_DIM]
            w_dt = jnp.zeros((d, LANES), BF16).at[:, :2 * SSM_HEADS].set(w_in[:, D_INNER + CONV_DIM:])
            z, xbc, dt_raw = _norm_mod_matmul(h, norm1_w[l], mod[l], 0, [w_z, w_xbc, w_dt], [BF16, BF16, F32], geo)
            yn = _ssd_mixer(z, xbc, dt_raw, ssd_conv_w[j], ssd_conv_b[j], ssd_dt_bias[j], ssd_a_log[j],
                            ssd_d[j], ssd_norm_w[j], geo)
            h = _matmul_residual(yn, ssd_out_w[j].astype(BF16), h, mod[l], 2, geo, latent_only=last)
        else:
            w_qkv = attn_qkv_w[j].astype(BF16)
            qd = N_Q_HEADS * HEAD_DIM
            kd = N_KV_HEADS * HEAD_DIM
            dup = lambda w: jnp.repeat(w.reshape(d, N_KV_HEADS, 1, HEAD_DIM), 2, axis=2).reshape(d, 2 * kd)
            w_q = w_qkv[:, :qd]
            w_k = dup(w_qkv[:, qd:qd + kd])
            w_v = dup(w_qkv[:, qd + kd:])
            qr, kdup, vdup = _norm_mod_matmul(h, norm1_w[l], mod[l], 0, [w_q, w_k, w_v], [BF16, BF16, BF16], geo,
                                              rope=(True, True, False), scales=(HEAD_DIM ** -0.5, 1.0, 1.0),
                                              tables=(cos, sin))
            o = _attention(qr, kdup, vdup, attn_sink[j], geo, latent_only=last)
            h = _matmul_residual(o, attn_out_w[j].astype(BF16), h, mod[l], 2, geo, latent_only=last)
        h = _moe(h, norm2_w[l], mod[l], router_hl, router_bias, l, moe_w_gate, moe_w_up, moe_w_down,
                 shared_w_gate[l].astype(BF16), shared_w_up[l].astype(BF16), shared_w_down[l].astype(BF16),
                 geo_lat if last else geo, final_w=final_norm_w if last else None)

    return h.reshape(bsz, seq, d)
```

```python
import functools

import jax
import jax.numpy as jnp
import numpy as np
from jax import lax
from jax.experimental import pallas as pl
from jax.experimental.pallas import tpu as pltpu

F32 = jnp.float32
BF16 = jnp.bfloat16

D_MODEL = 1024
DEPTH = 4
EPS = 1e-6
GRID_W = 64

D_INNER = 2048
SSM_HEADDIM = 64
SSM_HEADS = 32
SSM_STATE = 128
SSM_GROUPS = 4
SSM_CONV = 5
GN = SSM_GROUPS * SSM_STATE
CONV_DIM = D_INNER + 2 * GN
CHUNK = 128

HEAD_DIM = 64
N_Q_HEADS = 16
N_KV_HEADS = 4
ROPE_BASE = 10000.0
ROPE_PAIRS = 16
WINDOW = 128

N_EXPERTS = 16
N_EXPERT_GROUPS = 4
EXPERTS_PER_GROUP = 4
D_EXPERT = 512

TILE = 256
WIDE = 3
LANES = 128
MOD_ROWS = 16
NEG = -1e30
VMEM_LIMIT = 56 * 1024 * 1024


def _cp(n_axes, sem="arbitrary"):
    return pltpu.CompilerParams(dimension_semantics=(sem,) * n_axes, vmem_limit_bytes=VMEM_LIMIT)


def _dot(a, b):
    return jnp.dot(a, b, preferred_element_type=F32)


def _dot_nt(a, b):
    return lax.dot_general(a, b, (((1,), (1,)), ((), ())), preferred_element_type=F32)


def _split_bf16(v):
    hi = v.astype(BF16)
    lo = (v - hi.astype(F32)).astype(BF16)
    return hi, lo


def _sigmoid(x):
    return 1.0 / (1.0 + jnp.exp(-x))


def _silu(x):
    return x * _sigmoid(x)


def _softplus(x):
    return jnp.maximum(x, 0.0) + jnp.log(1.0 + jnp.exp(-jnp.abs(x)))


def _rms_mod(x, nw, shift, scale):
    y = x * lax.rsqrt(jnp.mean(x * x, axis=-1, keepdims=True) + EPS)
    return (y * nw) * (1.0 + scale) + shift


def _ada_kernel(c_ref, w_ref, b_ref, o_ref):
    s = _silu(c_ref[...])
    o_ref[...] = jnp.dot(s, w_ref[...], preferred_element_type=F32,
                         precision=lax.Precision.HIGHEST) + b_ref[...]


def _ada(cc, ada_w, ada_b):
    depth, d, n = ada_w.shape
    tn = 1024
    return pl.pallas_call(
        _ada_kernel,
        grid=(depth, n // tn),
        in_specs=[
            pl.BlockSpec((MOD_ROWS, d), lambda l, j: (0, 0)),
            pl.BlockSpec((None, d, tn), lambda l, j: (l, 0, j)),
            pl.BlockSpec((None, 1, tn), lambda l, j: (l, 0, j)),
        ],
        out_specs=pl.BlockSpec((None, MOD_ROWS, tn), lambda l, j: (l, 0, j)),
        out_shape=jax.ShapeDtypeStruct((depth, MOD_ROWS, n), F32),
        compiler_params=_cp(2),
        name="ada_table",
    )(cc, ada_w, ada_b.reshape(depth, 1, n))


def _rope128(x, cos, sin_signed, low_half):
    partner = jnp.where(low_half, pltpu.roll(x, LANES - ROPE_PAIRS, axis=1), pltpu.roll(x, ROPE_PAIRS, axis=1))
    return x * cos + partner * sin_signed


def _pick_ctx(step, rows, per_batch, ctx_val, batch_val):
    r = lax.broadcasted_iota(jnp.int32, (rows, 1), 0)
    is_ctx = jnp.logical_and(r < TILE, step % per_batch == 0)
    return jnp.where(is_ctx, ctx_val, batch_val)


def _nmm_kernel(*refs, n_out, rope, scales, per_batch):
    x_ref, nw_ref, sh_ref, sc_ref, shc_ref, scc_ref = refs[:6]
    w_refs = refs[6:6 + n_out]
    pos = 6 + n_out
    if any(rope):
        cos_ref, sin_ref = refs[pos:pos + 2]
        pos += 2
    o_refs = refs[pos:pos + n_out]
    step = pl.program_id(0)
    rows = x_ref.shape[0]
    shift = _pick_ctx(step, rows, per_batch, shc_ref[...], sh_ref[...])
    scale = _pick_ctx(step, rows, per_batch, scc_ref[...], sc_ref[...])
    a = _rms_mod(x_ref[...], nw_ref[...], shift, scale).astype(BF16)
    if any(rope):
        lane = lax.broadcasted_iota(jnp.int32, (rows, LANES), 1)
        low_half = (lane % (2 * ROPE_PAIRS)) < ROPE_PAIRS
        cos = cos_ref[...]
        sin = sin_ref[...]
    for k in range(n_out):
        if rope[k]:
            n = o_refs[k].shape[1]
            for j in range(n // LANES):
                acc = _dot(a, w_refs[k][:, j * LANES:(j + 1) * LANES])
                acc = _rope128(acc, cos, sin, low_half) * scales[k]
                o_refs[k][:, j * LANES:(j + 1) * LANES] = acc.astype(o_refs[k].dtype)
        else:
            o_refs[k][...] = (_dot(a, w_refs[k][...]) * scales[k]).astype(o_refs[k].dtype)


def _norm_mod_matmul(x, nw, mod, part, weights, out_dtypes, geo, rope=None, scales=None, tables=None):
    nt, d = x.shape
    n_out = len(weights)
    rope = tuple(rope) if rope is not None else (False,) * n_out
    scales = tuple(scales) if scales is not None else (1.0,) * n_out
    rows = WIDE * TILE
    per_batch = geo["tpb"] // WIDE
    bsz = geo["batch"]
    batch_mod = lambda p: pl.BlockSpec((None, 1, d), lambda i: (i // per_batch, 0, p))
    ctx_mod = lambda p: pl.BlockSpec((None, 1, d), lambda i: (bsz, 0, p))
    in_specs = [
        pl.BlockSpec((rows, d), lambda i: (i, 0)),
        pl.BlockSpec((1, d), lambda i: (0, 0)),
        batch_mod(part), batch_mod(part + 1), ctx_mod(part), ctx_mod(part + 1),
    ]
    args = [x, nw.reshape(1, d), mod, mod, mod, mod]
    for w in weights:
        in_specs.append(pl.BlockSpec(w.shape, lambda i: (0, 0), pipeline_mode=pl.Buffered(1)))
        args.append(w)
    if any(rope):
        in_specs += [pl.BlockSpec((rows, LANES), lambda i: (i % per_batch, 0))] * 2
        args += list(tables)
    return pl.pallas_call(
        functools.partial(_nmm_kernel, n_out=n_out, rope=rope, scales=scales, per_batch=per_batch),
        grid=(nt // rows,),
        in_specs=in_specs,
        out_specs=[pl.BlockSpec((rows, w.shape[1]), lambda i: (i, 0)) for w in weights],
        out_shape=[jax.ShapeDtypeStruct((nt, w.shape[1]), dt) for w, dt in zip(weights, out_dtypes)],
        compiler_params=_cp(1, "parallel"),
        name="norm_proj_rope" if any(rope) else "norm_proj",
    )(*args)


def _mmres_kernel(a_ref, w_ref, h_ref, g_ref, gc_ref, o_ref, *, per_batch):
    gate = g_ref[...]
    if per_batch is not None:
        gate = _pick_ctx(pl.program_id(0), a_ref.shape[0], per_batch, gc_ref[...], gate)
    o_ref[...] = h_ref[...] + gate * _dot(a_ref[...], w_ref[...])


def _matmul_residual(a, w, h, mod, part, geo, latent_only=False):
    nt, k = a.shape
    d = w.shape[1]
    bsz = geo["batch"]
    if latent_only:
        tpb, lat = geo["tpb"], geo["tpb"] - 1
        rows, n_tiles, per_batch = TILE, bsz * lat, None
        src = lambda i: (i // lat) * tpb + 1 + i % lat
        row = lambda i: i // lat
    else:
        rows = WIDE * TILE
        per_batch = geo["tpb"] // WIDE
        n_tiles = nt // rows
        src = lambda i: i
        row = lambda i: i // per_batch
    return pl.pallas_call(
        functools.partial(_mmres_kernel, per_batch=per_batch),
        grid=(n_tiles,),
        in_specs=[
            pl.BlockSpec((rows, k), lambda i: (src(i), 0)),
            pl.BlockSpec((k, d), lambda i: (0, 0), pipeline_mode=pl.Buffered(1)),
            pl.BlockSpec((rows, d), lambda i: (src(i), 0)),
            pl.BlockSpec((None, 1, d), lambda i: (row(i), 0, part)),
            pl.BlockSpec((None, 1, d), lambda i: (bsz, 0, part)),
        ],
        out_specs=pl.BlockSpec((rows, d), lambda i: (i, 0)),
        out_shape=jax.ShapeDtypeStruct((n_tiles * rows, d), F32),
        compiler_params=_cp(1, "parallel"),
        name="proj_residual",
    )(a, w, h, mod, mod)


def _ssd_chunk(x, bm, cm, dt, a_row, e_ref, state_ref, y_ref, direction):
    q = CHUNK
    ii = lax.broadcasted_iota(jnp.int32, (q, q), 0)
    jj = lax.broadcasted_iota(jnp.int32, (q, q), 1)
    tri = (jj <= ii) if direction == 0 else (jj >= ii)
    tri_b = jnp.where(tri, 1.0, 0.0).astype(BF16)
    da = dt * a_row
    da_hi, da_lo = _split_bf16(da)
    cum = _dot(tri_b, da_hi) + _dot(tri_b, da_lo)
    cum_t = cum.T
    e = e_ref[...]

    def expand(v):
        hi, lo = _split_bf16(v)
        return _dot(hi, e) + _dot(lo, e)

    dt_x = expand(dt)
    cum_x = expand(cum)
    edge = q - 1 if direction == 0 else 0
    tot_x = cum_x[edge:edge + 1, :]
    dtx = x * dt_x
    dtx_all = dtx.astype(BF16)
    to_end = (dtx * jnp.exp(tot_x - cum_x)).astype(BF16)
    from_start = jnp.exp(cum_x)
    chunk_decay = jnp.exp(tot_x)
    lane = lax.broadcasted_iota(jnp.int32, (q, LANES), 1)
    first_head = lane < SSM_HEADDIM
    gw = D_INNER // SSM_GROUPS
    for g in range(SSM_GROUPS):
        dtx_b = dtx_all[:, g * gw:(g + 1) * gw]
        bg = bm[:, g * SSM_STATE:(g + 1) * SSM_STATE]
        cg = cm[:, g * SSM_STATE:(g + 1) * SSM_STATE]
        cb = _dot_nt(cg, bg)
        h_prev = state_ref[g]
        y_off = _dot(cg, h_prev.astype(BF16)) * from_start[:, g * gw:(g + 1) * gw]
        bg_t = bg.astype(F32).T.astype(BF16)
        state_ref[g] = h_prev * chunk_decay[:, g * gw:(g + 1) * gw] + _dot(bg_t, to_end[:, g * gw:(g + 1) * gw])
        for p in range(gw // LANES):
            ms = []
            for k in range(2):
                hl = 32 * direction + g * 8 + 2 * p + k
                seg = cum[:, hl:hl + 1] - cum_t[hl:hl + 1, :]
                decay = jnp.exp(jnp.where(tri, seg, NEG))
                ms.append((cb * decay).astype(BF16))
            lo = g * gw + p * LANES
            out2 = _dot(jnp.concatenate(ms, axis=0), dtx_b[:, p * LANES:(p + 1) * LANES])
            y_diag = jnp.where(first_head, out2[:q], out2[q:])
            y_ref[:, lo:lo + LANES] = (y_diag + y_off[:, p * LANES:(p + 1) * LANES]).astype(y_ref.dtype)


def _ssd_bwd_kernel(cur_ref, prev_ref, next_ref, dt_ref, cw_ref, cbias_ref, dtb_ref, alog_ref, e_ref,
                    act_ref, yb_ref, state_ref, *, ncc, nch):
    s = pl.program_id(1)
    c = jnp.where(s < ncc, ncc - 1 - s, nch - 1 - (s - ncc))

    @pl.when(s == 0)
    def _():
        state_ref[...] = jnp.zeros_like(state_ref)

    has_prev = jnp.logical_and(c != 0, c != ncc)
    has_next = jnp.logical_and(c != ncc - 1, c != nch - 1)
    halo = prev_ref.shape[0]
    pad = (SSM_CONV - 1) // 2
    taps = [k for k in range(SSM_CONV) if k != pad]
    n_src = CHUNK + 2 * halo
    src = jnp.concatenate([prev_ref[...], cur_ref[...], next_ref[...]], axis=0)
    rr = lax.broadcasted_iota(jnp.int32, (len(taps) * CHUNK, n_src), 0)
    cc = lax.broadcasted_iota(jnp.int32, (len(taps) * CHUNK, n_src), 1)
    want = rr + (halo - pad)
    for n, k in enumerate(taps):
        want = jnp.where(rr >= n * CHUNK, rr - n * CHUNK + (halo - pad + k), want)
    inside = jnp.logical_and(jnp.logical_or(cc >= halo, has_prev), jnp.logical_or(cc < halo + CHUNK, has_next))
    shift = jnp.where(jnp.logical_and(cc == want, inside), 1.0, 0.0).astype(BF16)
    shifted = _dot(shift, src)
    acc = cbias_ref[...] + cw_ref[pad:pad + 1, :] * cur_ref[...].astype(F32)
    for n, k in enumerate(taps):
        acc = acc + cw_ref[k:k + 1, :] * shifted[n * CHUNK:(n + 1) * CHUNK]
    act = _silu(acc)
    act_b = act.astype(BF16)
    act_ref[...] = act_b

    dt = _softplus(dt_ref[...] + dtb_ref[...])
    a_row = -jnp.exp(alog_ref[...])
    _ssd_chunk(act[:, :D_INNER], act_b[:, D_INNER:D_INNER + GN], act_b[:, D_INNER + GN:], dt, a_row,
               e_ref, state_ref, yb_ref, 1)


def _ssd_fwd_kernel(act_ref, dt_ref, yb_ref, z_ref, dtb_ref, alog_ref, dskip_ref, nw_ref, e_ref,
                    o_ref, yf_ref, state_ref):
    s = pl.program_id(1)

    @pl.when(s == 0)
    def _():
        state_ref[...] = jnp.zeros_like(state_ref)

    act_b = act_ref[...]
    x = act_b[:, :D_INNER].astype(F32)
    dt = _softplus(dt_ref[...] + dtb_ref[...])
    a_row = -jnp.exp(alog_ref[...])
    _ssd_chunk(x, act_b[:, D_INNER:D_INNER + GN], act_b[:, D_INNER + GN:], dt, a_row, e_ref, state_ref, yf_ref, 0)
    y = yf_ref[...] + yb_ref[...] + x * dskip_ref[...]
    y = y * _silu(z_ref[...].astype(F32))
    y = y * lax.rsqrt(jnp.mean(y * y, axis=-1, keepdims=True) + EPS)
    o_ref[...] = (y * nw_ref[...]).astype(BF16)


def _ssd_mixer(z, xbc, dt_raw, conv_w, conv_b, dt_bias, a_log, d_skip, norm_w, geo):
    nt = z.shape[0]
    bsz, nch, ncc = geo["batch"], geo["nch"], geo["ncc"]
    halo = 16
    hb = CHUNK // halo
    n_halo_blocks = nt // halo

    def chunk_bwd(b, s):
        return b * nch + jnp.where(s < ncc, ncc - 1 - s, nch - 1 - (s - ncc))

    def chunk_fwd(b, s):
        return b * nch + s

    cw = jnp.zeros((8, CONV_DIM), F32).at[:SSM_CONV].set(conv_w)
    dtb = jnp.zeros((1, LANES), F32).at[0, :2 * SSM_HEADS].set(dt_bias.reshape(-1))
    alog = jnp.zeros((1, LANES), F32).at[0, :2 * SSM_HEADS].set(a_log.reshape(-1))
    heads = np.arange(D_INNER) // SSM_HEADDIM
    e_np = np.zeros((2, LANES, D_INNER), np.float32)
    for d in range(2):
        e_np[d, 32 * d + heads, np.arange(D_INNER)] = 1.0
    e_mats = jnp.asarray(e_np, BF16)
    const = lambda shape: pl.BlockSpec(shape, lambda b, s: (0,) * len(shape))

    act, yb = pl.pallas_call(
        functools.partial(_ssd_bwd_kernel, ncc=ncc, nch=nch),
        grid=(bsz, nch),
        in_specs=[
            pl.BlockSpec((CHUNK, CONV_DIM), lambda b, s: (chunk_bwd(b, s), 0)),
            pl.BlockSpec((halo, CONV_DIM), lambda b, s: (jnp.maximum(chunk_bwd(b, s) * hb - 1, 0), 0)),
            pl.BlockSpec((halo, CONV_DIM), lambda b, s: (jnp.minimum((chunk_bwd(b, s) + 1) * hb, n_halo_blocks - 1), 0)),
            pl.BlockSpec((CHUNK, LANES), lambda b, s: (chunk_bwd(b, s), 0)),
            const((8, CONV_DIM)),
            const((1, CONV_DIM)),
            const((1, LANES)),
            const((1, LANES)),
            pl.BlockSpec((None, LANES, D_INNER), lambda b, s: (1, 0, 0)),
        ],
        out_specs=[
            pl.BlockSpec((CHUNK, CONV_DIM), lambda b, s: (chunk_bwd(b, s), 0)),
            pl.BlockSpec((CHUNK, D_INNER), lambda b, s: (chunk_bwd(b, s), 0)),
        ],
        out_shape=[jax.ShapeDtypeStruct((nt, CONV_DIM), BF16), jax.ShapeDtypeStruct((nt, D_INNER), BF16)],
        scratch_shapes=[pltpu.VMEM((SSM_GROUPS, SSM_STATE, D_INNER // SSM_GROUPS), F32)],
        compiler_params=_cp(2),
        name="ssd_backward_pass",
    )(xbc, xbc, xbc, dt_raw, cw, conv_b.reshape(1, CONV_DIM), dtb, alog, e_mats)

    return pl.pallas_call(
        _ssd_fwd_kernel,
        grid=(bsz, nch),
        in_specs=[
            pl.BlockSpec((CHUNK, CONV_DIM), lambda b, s: (chunk_fwd(b, s), 0)),
            pl.BlockSpec((CHUNK, LANES), lambda b, s: (chunk_fwd(b, s), 0)),
            pl.BlockSpec((CHUNK, D_INNER), lambda b, s: (chunk_fwd(b, s), 0)),
            pl.BlockSpec((CHUNK, D_INNER), lambda b, s: (chunk_fwd(b, s), 0)),
            const((1, LANES)),
            const((1, LANES)),
            const((1, D_INNER)),
            const((1, D_INNER)),
            pl.BlockSpec((None, LANES, D_INNER), lambda b, s: (0, 0, 0)),
        ],
        out_specs=pl.BlockSpec((CHUNK, D_INNER), lambda b, s: (chunk_fwd(b, s), 0)),
        out_shape=jax.ShapeDtypeStruct((nt, D_INNER), BF16),
        scratch_shapes=[
            pltpu.VMEM((CHUNK, D_INNER), F32),
            pltpu.VMEM((SSM_GROUPS, SSM_STATE, D_INNER // SSM_GROUPS), F32),
        ],
        compiler_params=_cp(2),
        name="ssd_forward_pass",
    )(act, dt_raw, yb, z, dtb, alog, jnp.repeat(d_skip, SSM_HEADDIM).reshape(1, D_INNER),
      norm_w.reshape(1, D_INNER), e_mats)


def _attn_kernel(sink_ref, q_ref, kc_ref, vc_ref, kp_ref, ko_ref, kn_ref, vp_ref, vo_ref, vn_ref, o_ref,
                 *, ncc, n_lat_blocks, first):
    blk = pl.program_id(1) + first
    n = blk - ncc
    is_lat = n >= 0
    q = CHUNK
    ii = lax.broadcasted_iota(jnp.int32, (q, q), 0)
    jj = lax.broadcasted_iota(jnp.int32, (q, q), 1)
    ok_prev = jnp.logical_and(jj >= ii, jnp.logical_and(is_lat, n >= 1))
    ok_own = jnp.logical_and(jj >= 0, is_lat)
    ok_next = jnp.logical_and(jj <= ii, jnp.logical_and(is_lat, n <= n_lat_blocks - 2))
    n_ctx = kc_ref.shape[0]
    lane = lax.broadcasted_iota(jnp.int32, (q, LANES), 1)
    low = lane < HEAD_DIM
    zero = jnp.zeros((q, LANES), BF16)
    for kh in range(N_KV_HEADS):
        ks = slice(kh * LANES, (kh + 1) * LANES)
        kcat = jnp.concatenate([kc_ref[:, ks], kp_ref[:, ks], ko_ref[:, ks], kn_ref[:, ks]], axis=0)
        vcat = jnp.concatenate([vc_ref[:, ks], vp_ref[:, ks], vo_ref[:, ks], vn_ref[:, ks]], axis=0)
        lhs = []
        for m in range(2):
            qp = q_ref[:, (2 * kh + m) * LANES:(2 * kh + m + 1) * LANES]
            lhs += [jnp.where(low, qp, zero), jnp.where(low, zero, qp)]
        s_all = _dot_nt(jnp.concatenate(lhs, axis=0), kcat)
        ps, inv = [], []
        for gi in range(4):
            sink = sink_ref[kh * 4 + gi]
            sh = s_all[gi * q:(gi + 1) * q]
            sc = jnp.concatenate([sh[:, :n_ctx],
                                  jnp.where(ok_prev, sh[:, n_ctx:n_ctx + q], NEG),
                                  jnp.where(ok_own, sh[:, n_ctx + q:n_ctx + 2 * q], NEG),
                                  jnp.where(ok_next, sh[:, n_ctx + 2 * q:], NEG)], axis=1)
            mx = jnp.maximum(jnp.max(sc, axis=-1, keepdims=True), sink)
            p = jnp.exp(sc - mx)
            denom = jnp.sum(p, axis=-1, keepdims=True) + jnp.exp(sink - mx)
            ps.append(p.astype(BF16))
            inv.append(1.0 / denom)
        r = _dot(jnp.concatenate(ps, axis=0), vcat)
        for m in range(2):
            o = jnp.where(low, r[(2 * m) * q:(2 * m + 1) * q] * inv[2 * m],
                          r[(2 * m + 1) * q:(2 * m + 2) * q] * inv[2 * m + 1])
            o_ref[:, (2 * kh + m) * LANES:(2 * kh + m + 1) * LANES] = o.astype(BF16)


def _attention(qr, kd, vd, sink, geo, latent_only=False):
    nt = qr.shape[0]
    bsz, nch, ncc = geo["batch"], geo["nch"], geo["ncc"]
    nlb = nch - ncc
    kvw = kd.shape[1]
    ctx_rows = ncc * CHUNK
    first = ncc if latent_only else 0

    def win(o):
        return lambda b, j, *_: (b * nch + ncc + jnp.clip(j + first - ncc + o - 1, 0, nlb - 1), 0)

    grid_spec = pltpu.PrefetchScalarGridSpec(
        num_scalar_prefetch=1,
        grid=(bsz, nch - first),
        in_specs=[
            pl.BlockSpec((CHUNK, qr.shape[1]), lambda b, j, *_: (b * nch + first + j, 0)),
            pl.BlockSpec((ctx_rows, kvw), lambda b, j, *_: (b * (nch // ncc), 0)),
            pl.BlockSpec((ctx_rows, kvw), lambda b, j, *_: (b * (nch // ncc), 0)),
            pl.BlockSpec((CHUNK, kvw), win(0)),
            pl.BlockSpec((CHUNK, kvw), win(1)),
            pl.BlockSpec((CHUNK, kvw), win(2)),
            pl.BlockSpec((CHUNK, kvw), win(0)),
            pl.BlockSpec((CHUNK, kvw), win(1)),
            pl.BlockSpec((CHUNK, kvw), win(2)),
        ],
        out_specs=pl.BlockSpec((CHUNK, qr.shape[1]), lambda b, j, *_: (b * nch + first + j, 0)),
    )
    return pl.pallas_call(
        functools.partial(_attn_kernel, ncc=ncc, n_lat_blocks=nlb, first=first),
        grid_spec=grid_spec,
        out_shape=jax.ShapeDtypeStruct((nt, qr.shape[1]), BF16),
        compiler_params=_cp(2, "parallel"),
        name="window_attention",
    )(sink, qr, kd, vd, kd, kd, kd, vd, vd, vd)


RUN_ALIGN = 8
SORTED_ROWS = 2 * TILE + N_EXPERTS * RUN_ALIGN
RUN_BITS = 6


def _pack_bf16_pair(lo, hi):
    ulo = lax.bitcast_convert_type(lo, jnp.uint32)
    uhi = lax.bitcast_convert_type(hi, jnp.uint32)
    return (uhi & jnp.uint32(0xFFFF0000)) | (ulo >> 16)


def _unpack_bf16_pair(p):
    lo = lax.bitcast_convert_type(p << 16, F32)
    hi = lax.bitcast_convert_type(p & jnp.uint32(0xFFFF0000), F32)
    return lo.astype(BF16), hi.astype(BF16)


def _run_copies(meta, make_copy, wait):
    for e in range(N_EXPERTS):
        units = meta(e, 0) // RUN_ALIGN
        local = meta(e, 1)
        glob = meta(e, 2)
        for b in range(RUN_BITS - 1, -1, -1):
            size = RUN_ALIGN << b
            done = ((units >> (b + 1)) << (b + 1)) * RUN_ALIGN

            @pl.when(((units >> b) & 1) == 1)
            def _():
                cp = make_copy(pl.multiple_of(local + done, RUN_ALIGN), pl.multiple_of(glob + done, RUN_ALIGN), size)
                if wait:
                    cp.wait()
                else:
                    cp.start()


def _route_kernel(h_ref, nw_ref, sh_ref, sc_ref, rw_ref, rb_ref, wg_ref, wu_ref, wd_ref, xs_ref,
                  shared_ref, wcol_ref, meta_ref, cnt_ref,
                  sorted_ref, meta_v_ref, meta_s_ref, carry_ref, sem, *, cap):
    i = pl.program_id(0)
    last = pl.num_programs(0) - 1
    slot = i % 2
    t = TILE

    @pl.when(i == 0)
    def _():
        carry_ref[...] = jnp.zeros_like(carry_ref)

    m = _rms_mod(h_ref[...], nw_ref[...], sh_ref[...], sc_ref[...])
    mb, m_lo = _split_bf16(m)
    rw_hi = rw_ref[0]
    logits = (_dot(mb, rw_hi) + _dot(m_lo, rw_hi) + _dot(mb, rw_ref[1])).T[0:N_EXPERTS, :]
    hid = (_silu(_dot(mb, wg_ref[...])) * _dot(mb, wu_ref[...])).astype(BF16)
    shared_ref[...] = _dot(hid, wd_ref[...])
    scores = _sigmoid(logits)
    biased = scores + rb_ref[...]
    rows = [biased[e:e + 1, :] for e in range(N_EXPERTS)]
    srows = [scores[e:e + 1, :] for e in range(N_EXPERTS)]
    gscore = []
    for g in range(N_EXPERT_GROUPS):
        r = rows[g * 4:(g + 1) * 4]
        best = None
        for a in range(4):
            for b in range(a + 1, 4):
                pair = r[a] + r[b]
                best = pair if best is None else jnp.maximum(best, pair)
        gscore.append(best)
    gbest = jnp.maximum(jnp.maximum(gscore[0], gscore[1]), jnp.maximum(gscore[2], gscore[3]))
    gsel = jnp.full((1, t), N_EXPERT_GROUPS - 1, jnp.int32)
    for g in range(N_EXPERT_GROUPS - 2, -1, -1):
        gsel = jnp.where(gscore[g] == gbest, g, gsel)
    cand = [jnp.where(gsel == e // 4, rows[e], NEG) for e in range(N_EXPERTS)]
    best1 = functools.reduce(jnp.maximum, cand)
    e1 = jnp.full((1, t), N_EXPERTS - 1, jnp.int32)
    for e in range(N_EXPERTS - 2, -1, -1):
        e1 = jnp.where(cand[e] == best1, e, e1)
    cand2 = [jnp.where(e1 == e, NEG, cand[e]) for e in range(N_EXPERTS)]
    best2 = functools.reduce(jnp.maximum, cand2)
    e2 = jnp.full((1, t), N_EXPERTS - 1, jnp.int32)
    for e in range(N_EXPERTS - 2, -1, -1):
        e2 = jnp.where(cand2[e] == best2, e, e2)
    s1 = functools.reduce(jnp.add, [jnp.where(e1 == e, srows[e], 0.0) for e in range(N_EXPERTS)])
    s2 = functools.reduce(jnp.add, [jnp.where(e2 == e, srows[e], 0.0) for e in range(N_EXPERTS)])
    wsum = s1 + s2
    w1 = s1 / wsum
    w2 = s2 / wsum
    eid = lax.broadcasted_iota(jnp.int32, (N_EXPERTS, t), 0)
    oh1 = eid == e1
    oh2 = eid == e2
    onehot = jnp.where(jnp.logical_or(oh1, oh2), 1.0, 0.0)
    jr = lax.broadcasted_iota(jnp.int32, (t, t), 0)
    jc = lax.broadcasted_iota(jnp.int32, (t, t), 1)
    before = jnp.where(jr < jc, 1.0, 0.0).astype(BF16)
    prefix = _dot(onehot.astype(BF16), before)
    n_e = jnp.sum(onehot, axis=1, keepdims=True)
    run = jnp.floor((n_e + (RUN_ALIGN - 1)) * (1.0 / RUN_ALIGN)) * RUN_ALIGN
    run_b = jnp.broadcast_to(run, (N_EXPERTS, LANES))
    er = lax.broadcasted_iota(jnp.int32, (N_EXPERTS, N_EXPERTS), 0)
    ec = lax.broadcasted_iota(jnp.int32, (N_EXPERTS, N_EXPERTS), 1)
    lower = jnp.where(ec < er, 1.0, 0.0).astype(BF16)
    local = _dot(lower, run_b.astype(BF16))
    eid_l = lax.broadcasted_iota(jnp.int32, (N_EXPERTS, LANES), 0)
    glob = (eid_l * cap).astype(F32) + carry_ref[...]
    carry_ref[...] = carry_ref[...] + run_b
    rl1 = jnp.sum(jnp.where(oh1, prefix + local[:, 0:1], 0.0), axis=0, keepdims=True)
    rl2 = jnp.sum(jnp.where(oh2, prefix + local[:, 0:1], 0.0), axis=0, keepdims=True)
    lane_l = lax.broadcasted_iota(jnp.int32, (N_EXPERTS, LANES), 1)
    meta = jnp.where(lane_l == 0, run_b, jnp.where(lane_l == 1, local, jnp.where(lane_l == 2, glob, 0.0)))
    meta_ref[...] = meta.astype(jnp.int32)
    meta_v_ref[slot] = meta.astype(jnp.int32)
    cnt_ref[...] = carry_ref[...].astype(jnp.int32)
    wrows = jnp.concatenate([w1, w2, rl1, rl2, jnp.zeros((LANES - 4, t), F32)], axis=0)
    wcol_ref[...] = wrows.T

    meta_cp = pltpu.make_async_copy(meta_v_ref.at[slot], meta_s_ref.at[slot], sem.at[2 + slot])
    meta_cp.start()
    rr = lax.broadcasted_iota(jnp.int32, (SORTED_ROWS, t), 0)
    perm = jnp.where(jnp.logical_or(rr == rl1.astype(jnp.int32), rr == rl2.astype(jnp.int32)), 1.0, 0.0)
    srt = _dot(perm.astype(BF16), mb)
    half = sorted_ref.shape[2]
    sorted_ref[slot] = _pack_bf16_pair(srt[:, :half], srt[:, half:])

    def runs(s, wait):
        def make_copy(local_row, global_row, size):
            return pltpu.make_async_copy(sorted_ref.at[s, pl.ds(local_row, size)],
                                         xs_ref.at[pl.ds(global_row, size)], sem.at[s])
        _run_copies(lambda e, k: meta_s_ref[s, e, k], make_copy, wait)

    @pl.when(i > 0)
    def _():
        runs(1 - slot, True)

    meta_cp.wait()
    runs(slot, False)

    @pl.when(i == last)
    def _():
        runs(slot, True)


def _expert_kernel(blk_ref, exp_ref, nv_ref, new_ref, x_ref, wg_ref, wu_ref, wd_ref, y_ref, wgb_ref, wub_ref, wdb_ref):
    j = pl.program_id(0)
    nv = nv_ref[j]

    @pl.when(new_ref[j] == 1)
    def _():
        wgb_ref[...] = wg_ref[...].astype(BF16)
        wub_ref[...] = wu_ref[...].astype(BF16)
        wdb_ref[...] = wd_ref[...].astype(BF16)

    @pl.when(nv > 0)
    def _():
        rows = lax.broadcasted_iota(jnp.int32, (TILE, 1), 0)
        x_lo, x_hi = _unpack_bf16_pair(jnp.where(rows < nv, x_ref[...], jnp.uint32(0)))
        half = x_ref.shape[1]
        gate = _dot(x_lo, wgb_ref[0:half, :]) + _dot(x_hi, wgb_ref[half:, :])
        up = _dot(x_lo, wub_ref[0:half, :]) + _dot(x_hi, wub_ref[half:, :])
        hid = (_silu(gate) * up).astype(BF16)
        y = _dot(hid, wdb_ref[...]).astype(BF16).astype(F32)
        y_ref[...] = _pack_bf16_pair(y[:, :half], y[:, half:])

    @pl.when(nv <= 0)
    def _():
        y_ref[...] = jnp.zeros_like(y_ref)


def _combine_kernel(h_ref, shared_ref, g_ref, wcol_ref, meta0_ref, meta1_ref, meta2_ref, fw_ref, ys_ref, o_ref,
                    buf_ref, meta_v_ref, meta_s_ref, sem, *, final):
    t = TILE
    i = pl.program_id(0)
    n = pl.num_programs(0)
    n_gather_sems = 2

    def table_copy(s):
        return pltpu.make_async_copy(meta_v_ref.at[s], meta_s_ref.at[s], sem.at[n_gather_sems + s])

    def gather(table, slot, wait):
        def make_copy(local_row, global_row, size):
            return pltpu.make_async_copy(ys_ref.at[pl.ds(global_row, size)],
                                         buf_ref.at[slot, pl.ds(local_row, size)], sem.at[slot])
        _run_copies(lambda e, k: meta_s_ref[table, e, k], make_copy, wait)

    @pl.when(i == 0)
    def _():
        meta_v_ref[0] = meta0_ref[...]
        table_copy(0).start()
        table_copy(0).wait()
        gather(0, 0, False)
        meta_v_ref[1] = meta1_ref[...]
        table_copy(1).start()

    @pl.when(i + 1 < n)
    def _():
        table_copy((i + 1) % 3).wait()
        gather((i + 1) % 3, (i + 1) % 2, False)

    @pl.when(i + 2 < n)
    def _():
        meta_v_ref[(i + 2) % 3] = meta2_ref[...]
        table_copy((i + 2) % 3).start()

    cur = i % 3
    gather(cur, i % 2, True)
    total = meta_s_ref[cur, N_EXPERTS - 1, 0] + meta_s_ref[cur, N_EXPERTS - 1, 1]
    rows = lax.broadcasted_iota(jnp.int32, (SORTED_ROWS, 1), 0)
    y_lo, y_hi = _unpack_bf16_pair(jnp.where(rows < total, buf_ref[i % 2], jnp.uint32(0)))
    w = wcol_ref[...]
    cols = lax.broadcasted_iota(jnp.int32, (t, SORTED_ROWS), 1)
    pick1 = jnp.where(cols == w[:, 2:3].astype(jnp.int32), 1.0, 0.0).astype(BF16)
    pick2 = jnp.where(cols == w[:, 3:4].astype(jnp.int32), 1.0, 0.0).astype(BF16)
    routed = jnp.concatenate([w[:, 0:1] * _dot(pick1, y) + w[:, 1:2] * _dot(pick2, y) for y in (y_lo, y_hi)], axis=1)
    out = h_ref[...] + g_ref[...] * (shared_ref[...] + routed)
    if final:
        out = out * lax.rsqrt(jnp.mean(out * out, axis=-1, keepdims=True) + EPS) * fw_ref[...]
    o_ref[...] = out


def _moe(h, nw, mod, router_hl, router_b, layer, wg, wu, wd, swg, swu, swd, geo, final_w=None):
    nt, d = h.shape
    ntiles = nt // TILE
    row = geo["mod_row"]
    cap = -(-(nt + ntiles * (RUN_ALIGN - 1)) // TILE) * TILE
    dump_blk = N_EXPERTS * cap // TILE
    n_rows = N_EXPERTS * cap + TILE
    const = lambda shape: pl.BlockSpec(shape, lambda i: (0,) * len(shape))

    xs, shared, wcol, meta, cnt = pl.pallas_call(
        functools.partial(_route_kernel, cap=cap),
        grid=(ntiles,),
        in_specs=[
            pl.BlockSpec((TILE, d), lambda i: (i, 0)),
            const((1, d)),
            pl.BlockSpec((None, 1, d), lambda i: (row(i), 0, 3)),
            pl.BlockSpec((None, 1, d), lambda i: (row(i), 0, 4)),
            const((2, d, LANES)),
            const((N_EXPERTS, 1)),
            const(swg.shape), const(swu.shape), const(swd.shape),
        ],
        out_specs=[
            pl.BlockSpec(memory_space=pl.ANY),
            pl.BlockSpec((TILE, d), lambda i: (i, 0)),
            pl.BlockSpec((TILE, LANES), lambda i: (i, 0)),
            pl.BlockSpec((None, N_EXPERTS, LANES), lambda i: (i, 0, 0)),
            const((N_EXPERTS, LANES)),
        ],
        out_shape=[
            jax.ShapeDtypeStruct((n_rows, d // 2), jnp.uint32),
            jax.ShapeDtypeStruct((nt, d), F32),
            jax.ShapeDtypeStruct((nt, LANES), F32),
            jax.ShapeDtypeStruct((ntiles, N_EXPERTS, LANES), jnp.int32),
            jax.ShapeDtypeStruct((N_EXPERTS, LANES), jnp.int32),
        ],
        scratch_shapes=[
            pltpu.VMEM((2, SORTED_ROWS, d // 2), jnp.uint32),
            pltpu.VMEM((2, N_EXPERTS, LANES), jnp.int32),
            pltpu.SMEM((2, N_EXPERTS, LANES), jnp.int32),
            pltpu.VMEM((N_EXPERTS, LANES), F32),
            pltpu.SemaphoreType.DMA((4,)),
        ],
        compiler_params=_cp(1),
        name="moe_route",
    )(h, nw.reshape(1, d), mod, mod, router_hl, router_b.reshape(N_EXPERTS, 1), swg, swu, swd)

    counts = cnt[:, 0]
    tiles_e = (counts + TILE - 1) // TILE
    ends = jnp.cumsum(tiles_e)
    starts = ends - tiles_e
    n_sched = -(-(2 * nt + ntiles * N_EXPERTS * (RUN_ALIGN - 1)) // TILE) + N_EXPERTS
    jidx = jnp.arange(n_sched, dtype=jnp.int32)
    e_of = jnp.minimum(jnp.sum((jidx[:, None] >= ends[None, :]).astype(jnp.int32), axis=1), N_EXPERTS - 1)
    local = jidx - starts[e_of]
    active = jidx < ends[-1]
    last_e = e_of[jnp.maximum(ends[-1] - 1, 0)]
    tile_blk = jnp.where(active, e_of * (cap // TILE) + local, dump_blk).astype(jnp.int32)
    tile_e = jnp.where(active, e_of, last_e).astype(jnp.int32)
    tile_nv = jnp.where(active, jnp.minimum(counts[e_of] - local * TILE, TILE), 0).astype(jnp.int32)
    tile_new = jnp.logical_and(active, local == 0).astype(jnp.int32)

    ys = pl.pallas_call(
        _expert_kernel,
        grid_spec=pltpu.PrefetchScalarGridSpec(
            num_scalar_prefetch=4,
            grid=(n_sched,),
            in_specs=[
                pl.BlockSpec((TILE, d // 2), lambda j, blk, ex, nv, new: (blk[j], 0)),
                pl.BlockSpec((None, None, d, D_EXPERT), lambda j, blk, ex, nv, new: (layer, ex[j], 0, 0)),
                pl.BlockSpec((None, None, d, D_EXPERT), lambda j, blk, ex, nv, new: (layer, ex[j], 0, 0)),
                pl.BlockSpec((None, None, D_EXPERT, d), lambda j, blk, ex, nv, new: (layer, ex[j], 0, 0)),
            ],
            out_specs=pl.BlockSpec((TILE, d // 2), lambda j, blk, ex, nv, new: (blk[j], 0)),
            scratch_shapes=[
                pltpu.VMEM((d, D_EXPERT), BF16),
                pltpu.VMEM((d, D_EXPERT), BF16),
                pltpu.VMEM((D_EXPERT, d), BF16),
            ],
        ),
        out_shape=jax.ShapeDtypeStruct((n_rows, d // 2), jnp.uint32),
        compiler_params=_cp(1),
        name="moe_experts",
    )(tile_blk, tile_e, tile_nv, tile_new, xs, wg, wu, wd)

    final = final_w is not None
    fw = (final_w if final else jnp.ones((d,), F32)).reshape(1, d)
    return pl.pallas_call(
        functools.partial(_combine_kernel, final=final),
        grid=(ntiles,),
        in_specs=[
            pl.BlockSpec((TILE, d), lambda i: (i, 0)),
            pl.BlockSpec((TILE, d), lambda i: (i, 0)),
            pl.BlockSpec((None, 1, d), lambda i: (row(i), 0, 5)),
            pl.BlockSpec((TILE, LANES), lambda i: (i, 0)),
            pl.BlockSpec((None, N_EXPERTS, LANES), lambda i: (i, 0, 0)),
            pl.BlockSpec((None, N_EXPERTS, LANES), lambda i: (jnp.minimum(i + 1, ntiles - 1), 0, 0)),
            pl.BlockSpec((None, N_EXPERTS, LANES), lambda i: (jnp.minimum(i + 2, ntiles - 1), 0, 0)),
            pl.BlockSpec((1, d), lambda i: (0, 0)),
            pl.BlockSpec(memory_space=pl.ANY),
        ],
        out_specs=pl.BlockSpec((TILE, d), lambda i: (i, 0)),
        out_shape=jax.ShapeDtypeStruct((nt, d), F32),
        scratch_shapes=[
            pltpu.VMEM((2, SORTED_ROWS, d // 2), jnp.uint32),
            pltpu.VMEM((3, N_EXPERTS, LANES), jnp.int32),
            pltpu.SMEM((3, N_EXPERTS, LANES), jnp.int32),
            pltpu.SemaphoreType.DMA((5,)),
        ],
        compiler_params=_cp(1),
        name="moe_combine",
    )(h, shared, mod, wcol, meta, meta, meta, fw, ys)


def _rope_tables(n_ctx, seq):
    lane = np.arange(LANES) % HEAD_DIM
    axis = lane // (2 * ROPE_PAIRS)
    pair = lane % ROPE_PAIRS
    sign = np.where((lane % (2 * ROPE_PAIRS)) < ROPE_PAIRS, -1.0, 1.0).astype(np.float32)
    inv_freq = ROPE_BASE ** (-jnp.arange(ROPE_PAIRS, dtype=F32) / ROPE_PAIRS)
    t = jnp.arange(seq)
    posn = jnp.stack([t // GRID_W, t % GRID_W], axis=-1).astype(F32)
    ang = posn[:, axis] * inv_freq[pair][None, :]
    cos = jnp.concatenate([jnp.ones((n_ctx, LANES), F32), jnp.cos(ang)], axis=0)
    sin = jnp.concatenate([jnp.zeros((n_ctx, LANES), F32), jnp.sin(ang) * sign[None, :]], axis=0)
    return cos, sin


def kernel(x, c, ctx, c_ctx, ada_w, ada_b, norm1_w, norm2_w, ssd_in_w, ssd_conv_w, ssd_conv_b, ssd_dt_bias,
           ssd_a_log, ssd_d, ssd_norm_w, ssd_out_w, attn_qkv_w, attn_sink, attn_out_w, router_w, router_bias,
           moe_w_gate, moe_w_up, moe_w_down, shared_w_gate, shared_w_up, shared_w_down, final_norm_w):
    bsz, seq, d = x.shape
    n_ctx = ctx.shape[1]
    assert n_ctx == TILE and seq % TILE == 0 and d == D_MODEL and bsz < MOD_ROWS
    tpb = (n_ctx + seq) // TILE
    assert tpb % WIDE == 0
    geo = {
        "batch": bsz,
        "tpb": tpb,
        "nch": (n_ctx + seq) // CHUNK,
        "ncc": n_ctx // CHUNK,
        "mod_row": lambda i: jnp.where(i % tpb == 0, bsz, i // tpb),
    }
    nt = bsz * (n_ctx + seq)
    h = jnp.concatenate([ctx, x], axis=1).reshape(nt, d)

    cc = jnp.zeros((MOD_ROWS, d), F32).at[:bsz].set(c).at[bsz].set(c_ctx)
    mod = _ada(cc, ada_w, ada_b).reshape(DEPTH, MOD_ROWS, 1, 6 * d)
    cos, sin = _rope_tables(n_ctx, seq)
    rw_pad = jnp.zeros((d, LANES), F32).at[:, :N_EXPERTS].set(router_w)
    rw_hi = rw_pad.astype(BF16)
    router_hl = jnp.stack([rw_hi, (rw_pad - rw_hi.astype(F32)).astype(BF16)])

    geo_lat = dict(geo, mod_row=lambda i: i // (tpb - 1))
    for l in range(DEPTH):
        j = l // 2
        last = l == DEPTH - 1
        if l % 2 == 0:
            w_in = ssd_in_w[j].astype(BF16)
            w_z = w_in[:, :D_INNER]
            w_xbc = w_in[:, D_INNER:D_INNER + CONV_DIM]
            w_dt = jnp.zeros((d, LANES), BF16).at[:, :2 * SSM_HEADS].set(w_in[:, D_INNER + CONV_DIM:])
            z, xbc, dt_raw = _norm_mod_matmul(h, norm1_w[l], mod[l], 0, [w_z, w_xbc, w_dt], [BF16, BF16, F32], geo)
            yn = _ssd_mixer(z, xbc, dt_raw, ssd_conv_w[j], ssd_conv_b[j], ssd_dt_bias[j], ssd_a_log[j],
                            ssd_d[j], ssd_norm_w[j], geo)
            h = _matmul_residual(yn, ssd_out_w[j].astype(BF16), h, mod[l], 2, geo, latent_only=last)
        else:
            w_qkv = attn_qkv_w[j].astype(BF16)
            qd = N_Q_HEADS * HEAD_DIM
            kd = N_KV_HEADS * HEAD_DIM
            dup = lambda w: jnp.repeat(w.reshape(d, N_KV_HEADS, 1, HEAD_DIM), 2, axis=2).reshape(d, 2 * kd)
            w_q = w_qkv[:, :qd]
            w_k = dup(w_qkv[:, qd:qd + kd])
            w_v = dup(w_qkv[:, qd + kd:])
            qr, kdup, vdup = _norm_mod_matmul(h, norm1_w[l], mod[l], 0, [w_q, w_k, w_v], [BF16, BF16, BF16], geo,
                                              rope=(True, True, False), scales=(HEAD_DIM ** -0.5, 1.0, 1.0),
                                              tables=(cos, sin))
            o = _attention(qr, kdup, vdup, attn_sink[j], geo, latent_only=last)
            h = _matmul_residual(o, attn_out_w[j].astype(BF16), h, mod[l], 2, geo, latent_only=last)
        h = _moe(h, norm2_w[l], mod[l], router_hl, router_bias, l, moe_w_gate, moe_w_up, moe_w_down,
                 shared_w_gate[l].astype(BF16), shared_w_up[l].astype(BF16), shared_w_down[l].astype(BF16),
                 geo_lat if last else geo, final_w=final_norm_w if last else None)

    return h.reshape(bsz, seq, d)
```
